```python
import math
import numpy as np
import jax
import jax.numpy as jnp
from jax import lax

D_MODEL = 1024
BATCH = 1
SEQ = 16384
DEPTH = 2
DEC_BATCH = 32
DEC_SEQ = 8
PAST_LEN = 16384
PAGE_SIZE = 128

HEAD_DIM = 64
NSA_HEADS = 6
NSA_KV_HEADS = 2
NSA_GROUP = NSA_HEADS // NSA_KV_HEADS
CMP_STRIDE = 16
CMP_LEN = 2 * CMP_STRIDE
CMP_HIDDEN = 128
SEL_BLOCK = 64
SEL_RATIO = SEL_BLOCK // CMP_STRIDE
SEL_TOPK = 16
WINDOW = 512
DIFF_HEADS = 4
DIFF_QK = 32
DIFF_V = 64
DIFF_ROW = 2 * DIFF_QK + DIFF_V
LRU_WIDTH = 384
LRU_BLOCKS = 6
LRU_BLOCK = LRU_WIDTH // LRU_BLOCKS
CONV_W = 4
LRU_C = 8.0
D_NSA = NSA_HEADS * HEAD_DIM
D_DIFF = DIFF_HEADS * DIFF_V
D_MIX = D_NSA + D_DIFF + LRU_WIDTH
D_IN = D_NSA + 3 * NSA_HEADS + 6 * NSA_KV_HEADS * HEAD_DIM + 2 * DIFF_HEADS * 2 * DIFF_QK + D_DIFF + 2 * LRU_WIDTH
N_BUCKETS = 32
MAX_DISTANCE = 128
N_BIAS_HEADS = NSA_HEADS + DIFF_HEADS
D_FF = 2816
N_EXPERTS = 8
TOP_K = 2
D_FF_EXPERT = 1408
Q_BLOCK = 128
EPS = 1e-6
NEG_INF = -1e30
TINY = 1e-30
FORCE_SCORE = 1e4
ATT_SCALE = HEAD_DIM ** -0.5
DIFF_SCALE = DIFF_QK ** -0.5

kernel_name = 'hybrid_nsa_diff_rglru_step'


def _rmsnorm(x, g):
    xf = x.astype(jnp.float32)
    y = xf * lax.rsqrt(jnp.mean(xf * xf, axis=-1, keepdims=True) + EPS)
    return (y * g.astype(jnp.float32)).astype(x.dtype)


def _t5_bucket(dist):
    n = jnp.maximum(dist, 0)
    exact = N_BUCKETS // 2
    nf = jnp.maximum(n, 1).astype(jnp.float32)
    large = exact + (jnp.log(nf / exact) / math.log(MAX_DISTANCE / exact) * (N_BUCKETS - exact)).astype(jnp.int32)
    return jnp.where(n < exact, n, jnp.minimum(large, N_BUCKETS - 1))


def _masked_softmax(logits, mask):
    logits = jnp.where(mask, logits.astype(jnp.float32), NEG_INF)
    p = jnp.exp(logits - jnp.max(logits, axis=-1, keepdims=True)) * mask
    return p / jnp.maximum(jnp.sum(p, axis=-1, keepdims=True), TINY)


def _split_proj(proj):
    sizes = (D_NSA, 3 * NSA_HEADS, 6 * NSA_KV_HEADS * HEAD_DIM, 2 * DIFF_HEADS * DIFF_QK,
             2 * DIFF_HEADS * DIFF_QK, D_DIFF, LRU_WIDTH, LRU_WIDTH)
    return jnp.split(proj, np.cumsum(sizes)[:-1].tolist(), axis=-1)


def _compress(k, pe, w1, b1, w2):
    B, S = k.shape[:2]
    n_ch = S // CMP_STRIDE
    ch = k[:, :n_ch * CMP_STRIDE].reshape(B, n_ch, CMP_STRIDE, *k.shape[2:])
    first = ch[:, :-1] + pe[None, None, :CMP_STRIDE, None, :]
    second = ch[:, 1:] + pe[None, None, CMP_STRIDE:, None, :]
    hid = (jnp.einsum('bnlhd,lde->bnhe', first, w1[:CMP_STRIDE])
           + jnp.einsum('bnlhd,lde->bnhe', second, w1[CMP_STRIDE:]) + b1)
    return jnp.einsum('bnhe,ed->bnhd', jax.nn.gelu(hid), w2)


def _rglru(x, gate_in, conv_buf, h0, conv_w, conv_b, wa, ba, wx, bx, lam):
    B, T, D = x.shape
    xe = jnp.concatenate([conv_buf, x], axis=1)
    xc = conv_b + sum(xe[:, j:j + T] * conv_w[j] for j in range(CONV_W))
    xr = xc.reshape(B, T, LRU_BLOCKS, LRU_BLOCK)
    r = jax.nn.sigmoid(jnp.einsum('btnd,nde->btne', xr, wa).reshape(B, T, D) + ba)
    i = jax.nn.sigmoid(jnp.einsum('btnd,nde->btne', xr, wx).reshape(B, T, D) + bx)
    log_a = -LRU_C * r.astype(jnp.float32) * jax.nn.softplus(-lam.astype(jnp.float32))
    a = jnp.exp(log_a)
    b = jnp.sqrt(-jnp.expm1(2.0 * log_a)) * (i * xc).astype(jnp.float32)
    b = b.at[:, 0].add(a[:, 0] * h0.astype(jnp.float32))

    def combine(left, right):
        a1, b1 = left
        a2, b2 = right
        return a1 * a2, a2 * b1 + b2

    _, h = lax.associative_scan(combine, (a, b), axis=1)
    y = (h * jax.nn.gelu(gate_in.astype(jnp.float32))).astype(x.dtype)
    return y, h[:, -1].astype(x.dtype), xe[:, -(CONV_W - 1):]


def _mixer(h, past, p, layer_idx):
    B, T, _ = h.shape
    q_n, g_n, kv_n, q_d, k_d, v_d, x_r, g_r = _split_proj(h @ p['w_in'])
    q_n = q_n.reshape(B, T, NSA_KV_HEADS, NSA_GROUP, HEAD_DIM)
    g_n = jax.nn.sigmoid(g_n.reshape(B, T, NSA_KV_HEADS, NSA_GROUP, 3))
    kv_n = kv_n.reshape(B, T, 3, 2, NSA_KV_HEADS, HEAD_DIM)
    new_cmp, new_sel, new_win = kv_n[:, :, 0], kv_n[:, :, 1], kv_n[:, :, 2]
    q_d = q_d.reshape(B, T, DIFF_HEADS, 2, DIFF_QK)
    new_diff = jnp.concatenate([k_d.reshape(B, T, DIFF_HEADS, 2 * DIFF_QK),
                                v_d.reshape(B, T, DIFF_HEADS, DIFF_V)], axis=-1)

    cmp_full = jnp.concatenate([past['cmp'], new_cmp], axis=1)
    sel_full = jnp.concatenate([past['sel'], new_sel], axis=1)
    diff_full = jnp.concatenate([past['diff'], new_diff], axis=1)
    S = cmp_full.shape[1]
    pos0 = S - T

    kc = _compress(cmp_full[:, :, 0], p['cmp_pe'][0], p['cmp_w1'][0], p['cmp_b1'][0], p['cmp_w2'][0])
    vc = _compress(cmp_full[:, :, 1], p['cmp_pe'][1], p['cmp_w1'][1], p['cmp_b1'][1], p['cmp_w2'][1])
    n_cmp = kc.shape[1]
    c_end = jnp.arange(n_cmp, dtype=jnp.int32) * CMP_STRIDE + (CMP_LEN - 1)

    n_sel = -(-S // SEL_BLOCK)
    k_sel = min(SEL_TOPK, n_sel)
    sel_b = jnp.pad(sel_full, ((0, 0), (0, n_sel * SEL_BLOCK - S), (0, 0), (0, 0), (0, 0)))
    sel_b = sel_b.reshape(B, n_sel, SEL_BLOCK, 2, NSA_KV_HEADS, HEAD_DIM).transpose(3, 0, 4, 1, 2, 5)
    ks_b, vs_b = sel_b[0], sel_b[1]

    win_full = jnp.concatenate([past['win'], new_win], axis=1)
    n_keep = min(WINDOW, win_full.shape[1])
    win_pad = jnp.pad(win_full, ((0, 0), (WINDOW - past['win'].shape[1], 0), (0, 0), (0, 0), (0, 0)))
    kw, vw = win_pad[:, :, 0], win_pad[:, :, 1]

    kd1 = diff_full[..., :DIFF_QK]
    kd2 = diff_full[..., DIFF_QK:2 * DIFF_QK]
    vd = diff_full[..., 2 * DIFF_QK:]

    tab_nsa = p['rel_bias'][:, :NSA_HEADS].reshape(N_BUCKETS, NSA_KV_HEADS, NSA_GROUP)
    tab_hg = tab_nsa.transpose(1, 0, 2)
    tab_diff = p['rel_bias'][:, NSA_HEADS:]
    lam_init = 0.8 - 0.6 * math.exp(-0.3 * layer_idx)
    dl = p['diff_lambda'].astype(jnp.float32)
    lam = jnp.exp(jnp.sum(dl[0] * dl[1])) - jnp.exp(jnp.sum(dl[2] * dl[3])) + lam_init

    QB = Q_BLOCK if T % Q_BLOCK == 0 else T
    nb = T // QB
    bidx = jnp.arange(B)[:, None, None, None]
    hidx = jnp.arange(NSA_KV_HEADS)[None, None, :, None]
    key_pos = jnp.arange(S, dtype=jnp.int32)

    def block(args):
        qn, gn, q1, q2, b0 = args
        qp = pos0 + b0 + jnp.arange(QB, dtype=jnp.int32)
        dist_c = qp[:, None] - c_end[None, :]
        s_c = (jnp.einsum('bqhgd,bnhd->bqhgn', qn, kc) * ATT_SCALE
               + tab_nsa[_t5_bucket(dist_c)].transpose(0, 2, 3, 1)[None])
        p_c = _masked_softmax(s_c, (dist_c >= 0)[None, :, None, None, :])
        o_c = jnp.einsum('bqhgn,bnhd->bqhgd', p_c, vc)
        imp = jnp.sum(p_c, axis=3)
        imp = jnp.pad(imp, ((0, 0), (0, 0), (0, 0), (1, SEL_RATIO * (n_sel + 1) - 1 - n_cmp)))
        imp = imp.reshape(B, QB, NSA_KV_HEADS, n_sel + 1, SEL_RATIO)
        p_slc = jnp.sum(imp[..., :n_sel, :], axis=-1) + imp[..., 1:, 0]
        blk = jnp.arange(n_sel, dtype=jnp.int32)
        forced = (blk[None] == (qp // SEL_BLOCK)[:, None]) | (blk[None] == 0)
        allowed = blk[None] * SEL_BLOCK <= qp[:, None]
        score = jnp.where(forced[None, :, None, :], FORCE_SCORE,
                          jnp.where(allowed[None, :, None, :], p_slc, -FORCE_SCORE))
        _, idx = lax.top_k(score, k_sel)
        L = k_sel * SEL_BLOCK
        ks = ks_b[bidx, hidx, idx].reshape(B, QB, NSA_KV_HEADS, L, HEAD_DIM)
        vs = vs_b[bidx, hidx, idx].reshape(B, QB, NSA_KV_HEADS, L, HEAD_DIM)
        tok = (idx[..., None] * SEL_BLOCK + jnp.arange(SEL_BLOCK, dtype=jnp.int32)).reshape(B, QB, NSA_KV_HEADS, L)
        dist_s = qp[None, :, None, None] - tok
        bias_s = tab_hg[hidx, _t5_bucket(dist_s)]
        s_s = jnp.einsum('bqhgd,bqhld->bqhgl', qn, ks) * ATT_SCALE + jnp.swapaxes(bias_s, -1, -2)
        p_s = _masked_softmax(s_s, (dist_s >= 0)[:, :, :, None, :])
        o_s = jnp.einsum('bqhgl,bqhld->bqhgd', p_s, vs)
        kw_blk = lax.dynamic_slice_in_dim(kw, b0, WINDOW + QB, axis=1)
        vw_blk = lax.dynamic_slice_in_dim(vw, b0, WINDOW + QB, axis=1)
        kpos = pos0 - WINDOW + b0 + jnp.arange(WINDOW + QB, dtype=jnp.int32)
        dist_w = qp[:, None] - kpos[None, :]
        mask_w = (dist_w >= 0) & (dist_w < WINDOW) & (kpos >= 0)[None, :]
        s_w = (jnp.einsum('bqhgd,bkhd->bqhgk', qn, kw_blk) * ATT_SCALE
               + tab_nsa[_t5_bucket(dist_w)].transpose(0, 2, 3, 1)[None])
        p_w = _masked_softmax(s_w, mask_w[None, :, None, None, :])
        o_w = jnp.einsum('bqhgk,bkhd->bqhgd', p_w, vw_blk)
        o_nsa = gn[..., 0:1] * o_c + gn[..., 1:2] * o_s + gn[..., 2:3] * o_w
        dist_d = qp[:, None] - key_pos[None, :]
        bias_d = tab_diff[_t5_bucket(dist_d)].transpose(2, 0, 1)[None]
        mask_d = (dist_d >= 0)[None, None]
        a1 = _masked_softmax(jnp.einsum('bqhd,bkhd->bhqk', q1, kd1) * DIFF_SCALE + bias_d, mask_d)
        a2 = _masked_softmax(jnp.einsum('bqhd,bkhd->bhqk', q2, kd2) * DIFF_SCALE + bias_d, mask_d)
        o_d = jnp.einsum('bhqk,bkhd->bqhd', a1 - lam * a2, vd)
        return o_nsa, o_d

    def blocks(a):
        return a.reshape(B, nb, QB, *a.shape[2:]).swapaxes(0, 1)

    o_n, o_d = lax.map(block, (blocks(q_n), blocks(g_n), blocks(q_d[:, :, :, 0]), blocks(q_d[:, :, :, 1]),
                               jnp.arange(nb, dtype=jnp.int32) * QB))
    o_n = o_n.swapaxes(0, 1).reshape(B, T, D_NSA)
    o_d = o_d.swapaxes(0, 1).reshape(B, T, DIFF_HEADS, DIFF_V)
    o_d = (_rmsnorm(o_d, p['diff_subln_g']) * (1.0 - lam_init)).reshape(B, T, D_DIFF)

    y_r, h_last, conv_new = _rglru(x_r, g_r, past['conv'], past['h'], p['lru_conv_w'], p['lru_conv_b'],
                                   p['lru_wa'], p['lru_ba'], p['lru_wx'], p['lru_bx'], p['lru_lambda'])
    mix = jnp.concatenate([o_n.astype(h.dtype), o_d.astype(h.dtype), y_r], axis=-1) @ p['w_out']
    return mix, (new_cmp, new_sel, new_diff, win_full[:, -n_keep:], h_last, conv_new)


def _swiglu(h, w1, w3, w2):
    return (jax.nn.silu(h @ w1) * (h @ w3)) @ w2


def _moe(h, rw, rb, w1, w3, w2):
    logits = (h @ rw + rb).astype(jnp.float32)
    top_v, top_i = lax.top_k(logits, TOP_K)
    wts = jax.nn.softmax(top_v, axis=-1)
    gate = jnp.sum(jax.nn.one_hot(top_i, N_EXPERTS, dtype=jnp.float32) * wts[..., None], axis=-2)
    hid = jax.nn.silu(jnp.einsum('btd,edf->btef', h, w1)) * jnp.einsum('btd,edf->btef', h, w3)
    return jnp.einsum('btef,efd->btd', hid * gate[..., None].astype(hid.dtype), w2).astype(h.dtype)


def _layer(x, c, past, p, layer_idx):
    mod = jax.nn.silu(c) @ p['w_ada'] + p['b_ada']
    sh1, sc1, gt1, sh2, sc2, gt2 = [m[:, None, :] for m in jnp.split(mod, 6, axis=-1)]
    h = _rmsnorm(x, p['norm_mix_g']) * (1.0 + sc1) + sh1
    mix, state = _mixer(h, past, p, layer_idx)
    x = x + gt1 * mix
    h = _rmsnorm(x, p['norm_ffn_g']) * (1.0 + sc2) + sh2
    if layer_idx % 2 == 0:
        f = _swiglu(h, p['ffn_w1'], p['ffn_w3'], p['ffn_w2'])
    else:
        f = _moe(h, p['router_w'], p['router_b'], p['moe_w1'], p['moe_w3'], p['moe_w2'])
    return x + gt2 * f, state


def _trunk(x, c, pasts, layers, final_norm_g):
    states = []
    for l in range(DEPTH):
        x, st = _layer(x, c, pasts[l], layers[l], l)
        states.append(st)
    stacked = [jnp.stack([st[k] for st in states], axis=0) for k in range(6)]
    return _rmsnorm(x, final_norm_g), stacked


def _gather_pages(pool, l, page_table):
    g = pool[l, page_table]
    return g.reshape(page_table.shape[0], page_table.shape[1] * pool.shape[2], *pool.shape[3:])


def setup_inputs(seed: int = 0) -> dict:
    key = jax.random.key(seed)
    keys = iter(jax.random.split(key, 48))
    f32 = jnp.float32

    def nrm(shape, scale=1.0):
        return jax.random.normal(next(keys), shape, f32) * scale

    def gain(shape):
        return 1.0 + nrm(shape, 0.05)

    n_pages = PAST_LEN // PAGE_SIZE
    n_pool = (DEC_BATCH * n_pages * 5) // 4
    win_buf = min(WINDOW, PAST_LEN)
    n_dense = (DEPTH + 1) // 2
    n_moe = DEPTH // 2
    page_table = jax.random.permutation(next(keys), n_pool)[:DEC_BATCH * n_pages].reshape(DEC_BATCH, n_pages).astype(jnp.int32)
    u = jax.random.uniform(next(keys), (DEPTH, LRU_WIDTH), f32, 0.9, 0.999)
    a = u ** (1.0 / LRU_C)
    lru_lambda = jnp.log(a) - jnp.log1p(-a)
    D = D_MODEL
    return {
        'x_prompt': nrm((BATCH, SEQ, D)),
        'x_sample': nrm((DEC_BATCH, DEC_SEQ, D)),
        'c_prompt': nrm((BATCH, D)),
        'c_sample': nrm((DEC_BATCH, D)),
        'cache_nsa_cmp': nrm((DEPTH, n_pool, PAGE_SIZE, 2, NSA_KV_HEADS, HEAD_DIM)),
        'cache_nsa_sel': nrm((DEPTH, n_pool, PAGE_SIZE, 2, NSA_KV_HEADS, HEAD_DIM)),
        'cache_diff': nrm((DEPTH, n_pool, PAGE_SIZE, DIFF_HEADS, DIFF_ROW)),
        'cache_nsa_win': nrm((DEPTH, DEC_BATCH, win_buf, 2, NSA_KV_HEADS, HEAD_DIM)),
        'state_lru_h': nrm((DEPTH, DEC_BATCH, LRU_WIDTH), 0.5),
        'state_lru_conv': nrm((DEPTH, DEC_BATCH, CONV_W - 1, LRU_WIDTH)),
        'page_table': page_table,
        'rel_bias': nrm((N_BUCKETS, N_BIAS_HEADS), 0.5),
        'norm_mix_g': gain((DEPTH, D)),
        'norm_ffn_g': gain((DEPTH, D)),
        'final_norm_g': gain((D,)),
        'w_ada': nrm((DEPTH, D, 6 * D), D ** -0.5),
        'b_ada': nrm((DEPTH, 6 * D), 0.01),
        'w_in': nrm((DEPTH, D, D_IN), D ** -0.5),
        'cmp_pe': nrm((DEPTH, 2, CMP_LEN, HEAD_DIM), 0.5),
        'cmp_w1': nrm((DEPTH, 2, CMP_LEN, HEAD_DIM, CMP_HIDDEN), (CMP_LEN * HEAD_DIM) ** -0.5),
        'cmp_b1': nrm((DEPTH, 2, CMP_HIDDEN), 0.01),
        'cmp_w2': nrm((DEPTH, 2, CMP_HIDDEN, HEAD_DIM), CMP_HIDDEN ** -0.5),
        'diff_lambda': nrm((DEPTH, 4, DIFF_QK), 0.1),
        'diff_subln_g': gain((DEPTH, DIFF_V)),
        'lru_conv_w': nrm((DEPTH, CONV_W, LRU_WIDTH), CONV_W ** -0.5),
        'lru_conv_b': nrm((DEPTH, LRU_WIDTH), 0.01),
        'lru_wa': nrm((DEPTH, LRU_BLOCKS, LRU_BLOCK, LRU_BLOCK), LRU_BLOCK ** -0.5),
        'lru_ba': nrm((DEPTH, LRU_WIDTH), 0.01),
        'lru_wx': nrm((DEPTH, LRU_BLOCKS, LRU_BLOCK, LRU_BLOCK), LRU_BLOCK ** -0.5),
        'lru_bx': nrm((DEPTH, LRU_WIDTH), 0.01),
        'lru_lambda': lru_lambda,
        'w_out': nrm((DEPTH, D_MIX, D), D_MIX ** -0.5),
        'ffn_w1': nrm((n_dense, D, D_FF), D ** -0.5),
        'ffn_w3': nrm((n_dense, D, D_FF), D ** -0.5),
        'ffn_w2': nrm((n_dense, D_FF, D), D_FF ** -0.5),
        'router_w': nrm((n_moe, D, N_EXPERTS), D ** -0.5),
        'router_b': nrm((n_moe, N_EXPERTS), 0.01),
        'moe_w1': nrm((n_moe, N_EXPERTS, D, D_FF_EXPERT), D ** -0.5),
        'moe_w3': nrm((n_moe, N_EXPERTS, D, D_FF_EXPERT), D ** -0.5),
        'moe_w2': nrm((n_moe, N_EXPERTS, D_FF_EXPERT, D), D_FF_EXPERT ** -0.5),
    }


def reference(x_prompt, x_sample, c_prompt, c_sample, cache_nsa_cmp, cache_nsa_sel, cache_diff,
              cache_nsa_win, state_lru_h, state_lru_conv, page_table, rel_bias, norm_mix_g, norm_ffn_g,
              final_norm_g, w_ada, b_ada, w_in, cmp_pe, cmp_w1, cmp_b1, cmp_w2, diff_lambda, diff_subln_g,
              lru_conv_w, lru_conv_b, lru_wa, lru_ba, lru_wx, lru_bx, lru_lambda, w_out, ffn_w1, ffn_w3,
              ffn_w2, router_w, router_b, moe_w1, moe_w3, moe_w2):
    layers = []
    for l in range(DEPTH):
        p = {'rel_bias': rel_bias, 'norm_mix_g': norm_mix_g[l], 'norm_ffn_g': norm_ffn_g[l],
             'w_ada': w_ada[l], 'b_ada': b_ada[l], 'w_in': w_in[l], 'cmp_pe': cmp_pe[l],
             'cmp_w1': cmp_w1[l], 'cmp_b1': cmp_b1[l], 'cmp_w2': cmp_w2[l], 'diff_lambda': diff_lambda[l],
             'diff_subln_g': diff_subln_g[l], 'lru_conv_w': lru_conv_w[l], 'lru_conv_b': lru_conv_b[l],
             'lru_wa': lru_wa[l], 'lru_ba': lru_ba[l], 'lru_wx': lru_wx[l], 'lru_bx': lru_bx[l],
             'lru_lambda': lru_lambda[l], 'w_out': w_out[l]}
        j = l // 2
        if l % 2 == 0:
            p.update(ffn_w1=ffn_w1[j], ffn_w3=ffn_w3[j], ffn_w2=ffn_w2[j])
        else:
            p.update(router_w=router_w[j], router_b=router_b[j], moe_w1=moe_w1[j], moe_w3=moe_w3[j], moe_w2=moe_w2[j])
        layers.append(p)

    bp, dt = x_prompt.shape[0], x_prompt.dtype
    kv0 = jnp.zeros((bp, 0, 2, NSA_KV_HEADS, HEAD_DIM), dt)
    prompt_past = {'cmp': kv0, 'sel': kv0, 'diff': jnp.zeros((bp, 0, DIFF_HEADS, DIFF_ROW), dt), 'win': kv0,
                   'h': jnp.zeros((bp, LRU_WIDTH), dt), 'conv': jnp.zeros((bp, CONV_W - 1, LRU_WIDTH), dt)}
    sample_pasts = [{'cmp': _gather_pages(cache_nsa_cmp, l, page_table),
                     'sel': _gather_pages(cache_nsa_sel, l, page_table),
                     'diff': _gather_pages(cache_diff, l, page_table),
                     'win': cache_nsa_win[l], 'h': state_lru_h[l], 'conv': state_lru_conv[l]}
                    for l in range(DEPTH)]

    y_prompt, st_p = _trunk(x_prompt, c_prompt, [prompt_past] * DEPTH, layers, final_norm_g)
    y_sample, st_s = _trunk(x_sample, c_sample, sample_pasts, layers, final_norm_g)
    p_cmp, p_sel, p_diff, p_win, p_h, p_conv = st_p
    s_cmp, s_sel, s_diff, s_win, s_h, s_conv = st_s
    return (y_prompt, y_sample, p_cmp, p_sel, p_diff, p_win, p_h, p_conv,
            s_cmp, s_sel, s_diff, s_win, s_h, s_conv)
```

```python
import functools
import math

import numpy as np
import jax
import jax.numpy as jnp
from jax import lax
from jax.experimental import pallas as pl
from jax.experimental.pallas import tpu as pltpu

f32 = jnp.float32
bf16 = jnp.bfloat16

D_MODEL = 1024
HEAD_DIM = 64
NSA_HEADS = 6
NSA_KV_HEADS = 2
NSA_GROUP = 3
CMP_STRIDE = 16
CMP_LEN = 32
CMP_HIDDEN = 128
SEL_BLOCK = 64
SEL_RATIO = 4
SEL_TOPK = 16
WINDOW = 512
DIFF_HEADS = 4
DIFF_QK = 32
DIFF_V = 64
DIFF_ROW = 128
LRU_WIDTH = 384
LRU_BLOCKS = 6
LRU_BLOCK = 64
CONV_W = 4
LRU_C = 8.0
D_NSA = 384
D_DIFF = 256
N_BUCKETS = 32
MAX_DISTANCE = 128
D_FF = 2816
N_EXPERTS = 8
D_FF_EXPERT = 1408
EPS = 1e-6
NEG_INF = -1e30
TINY = 1e-30
FORCE_SCORE = 1e4
ATT_SCALE = HEAD_DIM ** -0.5
DIFF_SCALE = DIFF_QK ** -0.5
PAGE = 128

TQ = 256
M_INIT = -1e20
VMEM_LIMIT = 56 * 1024 * 1024

O_QN, O_QD, O_GATE, O_CMP, O_SEL, O_WIN, O_DIFF, O_XR, O_GR, O_SELAUG, O_WINB, N_PROJ = (
    0, 768, 1280, 1408, 1664, 1920, 2176, 2688, 3072, 3456, 3968, 4224)
S_QN, S_GN, S_CMP, S_SEL, S_WIN, S_QD, S_KD, S_VD, S_XR, S_GR, D_IN = (
    0, 384, 402, 658, 914, 1170, 1426, 1682, 1938, 2322, 2706)
N_OUT_IN = 768 + 512 + 384


def _proj_cols():
    c = []
    for h in range(NSA_HEADS):
        c += [S_QN + h * 64 + d for d in range(64)] + [-1] * 64
    for h in range(DIFF_HEADS):
        c += [S_QD + h * 64 + d for d in range(64)] + [-1] * 64
    c += [S_GN + i for i in range(18)] + [-1] * 110
    c += list(range(S_CMP, S_CMP + 256)) + list(range(S_SEL, S_SEL + 256)) + list(range(S_WIN, S_WIN + 256))
    for h in range(DIFF_HEADS):
        c += [S_KD + h * 64 + d for d in range(64)] + [S_VD + h * 64 + d for d in range(64)]
    c += list(range(S_XR, S_XR + 384)) + list(range(S_GR, S_GR + 384))
    for hk in range(NSA_KV_HEADS):
        c += [-1] * 128 + [S_SEL + hk * 64 + d for d in range(64)] + [S_SEL + 128 + hk * 64 + d for d in range(64)]
    for hk in range(NSA_KV_HEADS):
        c += [S_WIN + hk * 64 + d for d in range(64)] + [S_WIN + 128 + hk * 64 + d for d in range(64)]
    assert len(c) == N_PROJ
    return np.asarray(c, np.int32)


def _outproj_rows():
    r = []
    for h in range(NSA_HEADS):
        r += [-1] * 64 + [h * 64 + d for d in range(64)]
    for h in range(DIFF_HEADS):
        r += [-1] * 64 + [D_NSA + h * 64 + d for d in range(64)]
    r += list(range(D_NSA + D_DIFF, 1024))
    assert len(r) == N_OUT_IN
    return np.asarray(r, np.int32)


def _take_cols(w, cols):
    return jnp.where(cols[None, :] >= 0, jnp.take(w, np.maximum(cols, 0), axis=1), 0.0)


def _take_rows(w, rows):
    return jnp.where(rows[:, None] >= 0, jnp.take(w, np.maximum(rows, 0), axis=0), 0.0)


def _params(sem):
    return pltpu.CompilerParams(dimension_semantics=sem, vmem_limit_bytes=VMEM_LIMIT)


def _dot(a, b):
    return jnp.dot(a, b, preferred_element_type=f32)


def _dot_nt(a, b):
    return lax.dot_general(a, b, (((1,), (1,)), ((), ())), preferred_element_type=f32)


def _gelu(x):
    return 0.5 * x * (1.0 + jnp.tanh(math.sqrt(2.0 / math.pi) * (x + 0.044715 * (x * x * x))))


def _sigmoid(x):
    return 1.0 / (1.0 + jnp.exp(-x))


def _silu(x):
    return x * _sigmoid(x)


def _normmod(x, g, sc, sh):
    h = x * lax.rsqrt(jnp.mean(x * x, axis=-1, keepdims=True) + EPS) * g
    return h * (1.0 + sc) + sh


def _rowspec(rows, tm, width):
    if rows == 1:
        return pl.BlockSpec((1, width), lambda i: (0, 0))
    return pl.BlockSpec((tm, width), lambda i: (i, 0))


def _ada_body(c_ref, w_ref, b_ref, o_ref):
    c = c_ref[...]
    o_ref[...] = _dot(_silu(c).astype(bf16), w_ref[...].astype(bf16)) + b_ref[...]


def _ada(c, w, b):
    m, n = c.shape[0], w.shape[1]
    tn = 1536
    return pl.pallas_call(
        _ada_body,
        grid=(n // tn,),
        in_specs=[pl.BlockSpec((m, D_MODEL), lambda j: (0, 0)),
                  pl.BlockSpec((D_MODEL, tn), lambda j: (0, j)),
                  pl.BlockSpec((1, tn), lambda j: (0, j))],
        out_specs=pl.BlockSpec((m, tn), lambda j: (0, j)),
        out_shape=jax.ShapeDtypeStruct((m, n), f32),
        compiler_params=_params(("parallel",)),
        name="ada",
    )(c, w, b.reshape(1, n))


def _proj_body(tm, x_ref, g_ref, sc_ref, sh_ref, w_ref, qn_ref, qd_ref, gate_ref, cmp_ref, sel_ref, win_ref,
               diff_ref, diffb_ref, xr_ref, gr_ref, selaug_ref, winb_ref):
    h = _normmod(x_ref[...], g_ref[...], sc_ref[...], sh_ref[...])
    pr = _dot(h.astype(bf16), w_ref[...])
    qn_ref[...] = (pr[:, O_QN:O_QD] * ATT_SCALE).astype(bf16)
    qd_ref[...] = (pr[:, O_QD:O_GATE] * DIFF_SCALE).astype(bf16)
    gate_ref[...] = _sigmoid(pr[:, O_GATE:O_CMP])
    cmp_ref[...] = pr[:, O_CMP:O_SEL]
    sel_ref[...] = pr[:, O_SEL:O_WIN]
    win_ref[...] = pr[:, O_WIN:O_DIFF]
    d = pr[:, O_DIFF:O_XR]
    diff_ref[...] = d
    diffb_ref[...] = d.astype(bf16)
    xr_ref[...] = pr[:, O_XR:O_GR]
    gr_ref[...] = pr[:, O_GR:O_SELAUG]
    t = pl.program_id(0) * tm + lax.broadcasted_iota(jnp.int32, (tm, 128), 0)
    onehot = (lax.broadcasted_iota(jnp.int32, (tm, 128), 1) == (t // SEL_BLOCK) % 128).astype(f32)
    for hk in range(NSA_KV_HEADS):
        o = O_SELAUG + hk * 256
        selaug_ref[:, hk * 256:hk * 256 + 128] = onehot.astype(bf16)
        selaug_ref[:, hk * 256 + 128:(hk + 1) * 256] = pr[:, o + 128:o + 256].astype(bf16)
    winb_ref[...] = pr[:, O_WINB:N_PROJ].astype(bf16)


def _proj(x, g, sc, sh, w, tm):
    m = x.shape[0]
    widths = [(768, bf16), (512, bf16), (128, f32), (256, f32), (256, f32), (256, f32), (512, f32), (512, bf16),
              (384, f32), (384, f32), (512, bf16), (256, bf16)]
    return pl.pallas_call(
        functools.partial(_proj_body, tm),
        grid=(m // tm,),
        in_specs=[pl.BlockSpec((tm, D_MODEL), lambda i: (i, 0)),
                  pl.BlockSpec((1, D_MODEL), lambda i: (0, 0)),
                  _rowspec(sc.shape[0], tm, D_MODEL), _rowspec(sh.shape[0], tm, D_MODEL),
                  pl.BlockSpec((D_MODEL, N_PROJ), lambda i: (0, 0))],
        out_specs=[pl.BlockSpec((tm, wd), lambda i: (i, 0)) for wd, _ in widths],
        out_shape=[jax.ShapeDtypeStruct((m, wd), dt) for wd, dt in widths],
        compiler_params=_params(("parallel",)),
        name="proj",
    )(x, g.reshape(1, -1), sc, sh, w)


CMP_PAGES = 16
CMP_ROWS = CMP_PAGES * 8


def _compress_body(*refs):
    pages = refs[1:1 + CMP_PAGES]
    wc_ref, pe_ref, b1_ref, w2_ref, o_ref, carry_ref = refs[1 + CMP_PAGES:]
    g = pl.program_id(1)

    @pl.when(g == 0)
    def _():
        carry_ref[...] = jnp.zeros_like(carry_ref)

    x = jnp.concatenate([p[0, 0] for p in pages], axis=0).astype(bf16)
    a = _dot(x, wc_ref[...])
    pe = _dot(pe_ref[...].astype(bf16), wc_ref[...])
    const = pe[0:1, :512] + pe[1:2, 512:] + b1_ref[...]
    first, second = a[:, :512], a[:, 512:]
    rid = lax.broadcasted_iota(jnp.int32, (CMP_ROWS, 512), 0)
    prev_first = jnp.where(rid == 0, carry_ref[0:1, :], pltpu.roll(first, 1, 0))
    carry_ref[0:1, :] = first[CMP_ROWS - 1:CMP_ROWS, :]
    hid = _gelu(prev_first + second + const)
    o_ref[0] = _dot(hid.astype(bf16), w2_ref[...]).astype(bf16)


def _compress(pool, table, wc, pe2, b1, w2bd):
    b, n_pages = table.shape
    steps = n_pages // CMP_PAGES

    def page_spec(i):
        return pl.BlockSpec((1, 1, 8, 4096), lambda bb, g, tab: (tab[bb, g * CMP_PAGES + i], 0, 0, 0))

    gs = pltpu.PrefetchScalarGridSpec(
        num_scalar_prefetch=1,
        grid=(b, steps),
        in_specs=[page_spec(i) for i in range(CMP_PAGES)] + [
            pl.BlockSpec((4096, 1024), lambda bb, g, tab: (0, 0)),
            pl.BlockSpec((8, 4096), lambda bb, g, tab: (0, 0)),
            pl.BlockSpec((1, 512), lambda bb, g, tab: (0, 0)),
            pl.BlockSpec((512, 256), lambda bb, g, tab: (0, 0))],
        out_specs=pl.BlockSpec((1, CMP_ROWS, 256), lambda bb, g, tab: (bb, g, 0)),
        scratch_shapes=[pltpu.VMEM((8, 512), f32)],
    )
    pool4 = pool.reshape(pool.shape[0], 1, 8, 4096)
    return pl.pallas_call(
        _compress_body,
        grid_spec=gs,
        out_shape=jax.ShapeDtypeStruct((b, n_pages * 8, 256), bf16),
        compiler_params=_params(("parallel", "arbitrary")),
        name="compress",
    )(table, *([pool4] * CMP_PAGES), wc, pe2, b1, w2bd)


def _compress_weights(cmp_pe, cmp_w1, cmp_b1, cmp_w2):
    w1 = cmp_w1.reshape(2, 2, CMP_STRIDE, HEAD_DIM, CMP_HIDDEN)
    eye4 = jnp.eye(4, dtype=f32)
    w1s = jnp.stack([w1[0], w1[0], w1[1], w1[1]], axis=0)
    wc = jnp.einsum('sflde,st->lsdfte', w1s, eye4).reshape(4096, 1024)
    pe = cmp_pe.reshape(2, 2, CMP_STRIDE, HEAD_DIM)
    pes = jnp.stack([pe[0], pe[0], pe[1], pe[1]], axis=0)
    pe2 = jnp.transpose(pes, (1, 2, 0, 3)).reshape(2, 4096)
    pe2 = jnp.concatenate([pe2, jnp.zeros((6, 4096), f32)], axis=0)
    b1 = jnp.stack([cmp_b1[0], cmp_b1[0], cmp_b1[1], cmp_b1[1]], axis=0).reshape(1, 512)
    w2bd = jnp.zeros((512, 256), f32)
    for s, slot in ((0, 0), (2, 1), (1, 2), (3, 3)):
        w2bd = w2bd.at[s * 128:(s + 1) * 128, slot * 64:(slot + 1) * 64].set(cmp_w2[s // 2])
    return wc.astype(bf16), pe2, b1, w2bd.astype(bf16)


def _flash_step(s, cb, vtile, m_ref, l_ref, acc_ref):
    m_prev = m_ref[...]
    m_new = jnp.maximum(m_prev, jnp.max(s, axis=1, keepdims=True) + cb)
    alpha = jnp.exp(m_prev - m_new)
    p = jnp.exp(s - (m_new - cb))
    l_ref[...] = alpha * l_ref[...] + jnp.sum(p, axis=1, keepdims=True)
    acc_ref[...] = alpha * acc_ref[...] + _dot(p.astype(bf16), vtile)
    m_ref[...] = m_new


def _init_state(m_ref, l_ref, acc_ref):
    m_ref[...] = jnp.full(m_ref.shape, M_INIT, f32)
    l_ref[...] = jnp.zeros(l_ref.shape, f32)
    acc_ref[...] = jnp.zeros(acc_ref.shape, f32)


def _hi_lane_mask(rows):
    return lax.broadcasted_iota(jnp.int32, (rows, 128), 1) >= 64


def _cmp_body(qn_ref, gate_ref, slab_ref, bd_ref, m5_ref, oc_ref, mneg_ref):
    i = pl.program_id(0)
    n_cmp = slab_ref.shape[0]
    t = i * TQ + lax.broadcasted_iota(jnp.int32, (TQ, n_cmp), 0)
    n = lax.broadcasted_iota(jnp.int32, (TQ, n_cmp), 1)
    dist = t - (CMP_STRIDE * n + CMP_STRIDE - 1)
    valid = (dist >= 0) & (n >= 1)
    idx = jnp.clip(dist, 0, 127)
    hi = _hi_lane_mask(TQ)
    blk = lax.broadcasted_iota(jnp.int32, (TQ, 256), 1)
    tq = i * TQ + lax.broadcasted_iota(jnp.int32, (TQ, 256), 0)
    forced = (blk == tq // SEL_BLOCK) | (blk == 0)
    allowed = blk * SEL_BLOCK <= tq
    for hk in range(NSA_KV_HEADS):
        slab = slab_ref[:, hk * 128:(hk + 1) * 128]
        imp = jnp.zeros((TQ, n_cmp), f32)
        for g in range(NSA_GROUP):
            h = hk * NSA_GROUP + g
            s = _dot_nt(qn_ref[:, h * 128:(h + 1) * 128], slab)
            tab = jnp.broadcast_to(bd_ref[h:h + 1, :], (TQ, 128))
            bias = jnp.concatenate(
                [jnp.take_along_axis(tab, idx[:, c * 128:(c + 1) * 128], axis=1) for c in range(n_cmp // 128)],
                axis=1)
            s = jnp.where(valid, s + bias, NEG_INF)
            p = jnp.exp(s - jnp.max(s, axis=1, keepdims=True)) * valid.astype(f32)
            p = p / jnp.maximum(jnp.sum(p, axis=1, keepdims=True), TINY)
            imp = imp + p
            o = _dot(p.astype(bf16), slab)
            gate = gate_ref[:, h * 3:h * 3 + 1]
            oc_ref[:, h * 128:(h + 1) * 128] = jnp.where(hi, o * gate, 0.0)
        imp_hi = imp.astype(bf16)
        imp_lo = (imp - imp_hi.astype(f32)).astype(bf16)
        p_slc = _dot(imp_hi, m5_ref[...]) + _dot(imp_lo, m5_ref[...])
        score = jnp.where(forced, FORCE_SCORE, jnp.where(allowed, p_slc, -FORCE_SCORE))
        chosen = jnp.zeros((TQ, 256), jnp.bool_)
        for _ in range(SEL_TOPK):
            mx = jnp.max(score, axis=1, keepdims=True)
            first = jnp.min(jnp.where(score == mx, blk, 4096), axis=1, keepdims=True)
            pick = blk == first
            chosen = chosen | pick
            score = jnp.where(pick, -3e38, score)
        mneg_ref[:, hk * 256:(hk + 1) * 256] = jnp.where(chosen, 0.0, NEG_INF).astype(bf16)


def _cmp_attn(qn, gates, slab, bd, m5):
    t = qn.shape[0]
    n_cmp = slab.shape[0]
    return pl.pallas_call(
        _cmp_body,
        grid=(t // TQ,),
        in_specs=[pl.BlockSpec((TQ, 768), lambda i: (i, 0)),
                  pl.BlockSpec((TQ, 128), lambda i: (i, 0)),
                  pl.BlockSpec((n_cmp, 256), lambda i: (0, 0)),
                  pl.BlockSpec((16, 128), lambda i: (0, 0)),
                  pl.BlockSpec((n_cmp, 256), lambda i: (0, 0))],
        out_specs=[pl.BlockSpec((TQ, 768), lambda i: (i, 0)),
                   pl.BlockSpec((TQ, 512), lambda i: (i, 0))],
        out_shape=[jax.ShapeDtypeStruct((t, 768), f32), jax.ShapeDtypeStruct((t, 512), bf16)],
        compiler_params=_params(("parallel",)),
        name="cmp_attn",
    )(qn, gates, slab, bd, m5)


def _sel_body(qn_ref, mneg_ref, gate_ref, kaug_ref, d0_ref, d1_ref, far_ref, os_ref, m_ref, l_ref, acc_ref):
    i = pl.program_id(0)
    hi = _hi_lane_mask(TQ)
    rows = NSA_GROUP * TQ
    for hk in range(NSA_KV_HEADS):
        _init_state(m_ref, l_ref, acc_ref)
        q = jnp.concatenate([qn_ref[:, (hk * NSA_GROUP + g) * 128:(hk * NSA_GROUP + g + 1) * 128]
                             for g in range(NSA_GROUP)], axis=0)
        mn = mneg_ref[:, hk * 256:(hk + 1) * 256]
        lhs_a = jnp.concatenate([jnp.concatenate([mn[:, :128]] * NSA_GROUP, axis=0), q], axis=1)
        lhs_b = jnp.concatenate([jnp.concatenate([mn[:, 128:]] * NSA_GROUP, axis=0), q], axis=1)
        cb = jnp.concatenate([jnp.broadcast_to(far_ref[0:1, hk * NSA_GROUP + g:hk * NSA_GROUP + g + 1], (TQ, 1))
                              for g in range(NSA_GROUP)], axis=0)

        def tile(j, lhs, bias):
            kt = kaug_ref[pl.ds(pl.multiple_of(j * TQ, TQ), TQ), hk * 256:(hk + 1) * 256]
            s = _dot_nt(lhs, kt)
            if bias is None:
                _flash_step(s, cb, kt, m_ref, l_ref, acc_ref)
            else:
                s = s + jnp.concatenate([bias[hk * NSA_GROUP + g] for g in range(NSA_GROUP)], axis=0)
                _flash_step(s, 0.0, kt, m_ref, l_ref, acc_ref)

        n_far = jnp.maximum(i - 1, 0)
        split = 128 * SEL_BLOCK // TQ

        def far_a(j, c):
            tile(j, lhs_a, None)
            return c

        def far_b(j, c):
            tile(j, lhs_b, None)
            return c

        lax.fori_loop(0, jnp.minimum(n_far, split), far_a, 0)
        lax.fori_loop(split, jnp.maximum(n_far, split), far_b, 0)

        @pl.when(i >= 1)
        def _():
            tile(i - 1, jnp.where(i - 1 < split, lhs_a, lhs_b), d1_ref)

        tile(i, jnp.where(i < split, lhs_a, lhs_b), d0_ref)
        o = acc_ref[:, 128:256] / jnp.maximum(l_ref[...], TINY)
        for g in range(NSA_GROUP):
            h = hk * NSA_GROUP + g
            gate = gate_ref[:, h * 3 + 1:h * 3 + 2]
            os_ref[:, h * 128:(h + 1) * 128] = jnp.where(hi, o[g * TQ:(g + 1) * TQ] * gate, 0.0)


def _sel_attn(qn, mneg, gates, kaug, d0, d1, far):
    t = qn.shape[0]
    rows = NSA_GROUP * TQ
    return pl.pallas_call(
        _sel_body,
        grid=(t // TQ,),
        in_specs=[pl.BlockSpec((TQ, 768), lambda i: (i, 0)),
                  pl.BlockSpec((TQ, 512), lambda i: (i, 0)),
                  pl.BlockSpec((TQ, 128), lambda i: (i, 0)),
                  pl.BlockSpec((t, 512), lambda i: (0, 0)),
                  pl.BlockSpec((NSA_HEADS, TQ, TQ), lambda i: (0, 0, 0)),
                  pl.BlockSpec((NSA_HEADS, TQ, TQ), lambda i: (0, 0, 0)),
                  pl.BlockSpec((8, 128), lambda i: (0, 0))],
        out_specs=pl.BlockSpec((TQ, 768), lambda i: (i, 0)),
        out_shape=jax.ShapeDtypeStruct((t, 768), f32),
        scratch_shapes=[pltpu.VMEM((rows, 1), f32), pltpu.VMEM((rows, 1), f32), pltpu.VMEM((rows, 256), f32)],
        compiler_params=_params(("parallel",)),
        name="sel_attn",
    )(qn, mneg, gates, kaug, d0, d1, far)


def _win_body(qn_ref, gate_ref, kv_ref, d0_ref, d1_ref, d2_ref, ow_ref, m_ref, l_ref, acc_ref):
    i = pl.program_id(0)
    hi = _hi_lane_mask(TQ)
    for hk in range(NSA_KV_HEADS):
        _init_state(m_ref, l_ref, acc_ref)
        q = jnp.concatenate([qn_ref[:, (hk * NSA_GROUP + g) * 128:(hk * NSA_GROUP + g + 1) * 128]
                             for g in range(NSA_GROUP)], axis=0)

        def tile(j, bias):
            kt = kv_ref[pl.ds(pl.multiple_of(j * TQ, TQ), TQ), hk * 128:(hk + 1) * 128]
            s = _dot_nt(q, kt) + jnp.concatenate([bias[hk * NSA_GROUP + g] for g in range(NSA_GROUP)], axis=0)
            _flash_step(s, 0.0, kt, m_ref, l_ref, acc_ref)

        @pl.when(i >= 2)
        def _():
            tile(i - 2, d2_ref)

        @pl.when(i >= 1)
        def _():
            tile(i - 1, d1_ref)

        tile(i, d0_ref)
        o = acc_ref[...] / jnp.maximum(l_ref[...], TINY)
        for g in range(NSA_GROUP):
            h = hk * NSA_GROUP + g
            gate = gate_ref[:, h * 3 + 2:h * 3 + 3]
            ow_ref[:, h * 128:(h + 1) * 128] = jnp.where(hi, o[g * TQ:(g + 1) * TQ] * gate, 0.0)


def _win_attn(qn, gates, kv, d0, d1, d2):
    t = qn.shape[0]
    rows = NSA_GROUP * TQ
    tile_spec = pl.BlockSpec((NSA_HEADS, TQ, TQ), lambda i: (0, 0, 0))
    return pl.pallas_call(
        _win_body,
        grid=(t // TQ,),
        in_specs=[pl.BlockSpec((TQ, 768), lambda i: (i, 0)),
                  pl.BlockSpec((TQ, 128), lambda i: (i, 0)),
                  pl.BlockSpec((t, 256), lambda i: (0, 0)),
                  tile_spec, tile_spec, tile_spec],
        out_specs=pl.BlockSpec((TQ, 768), lambda i: (i, 0)),
        out_shape=jax.ShapeDtypeStruct((t, 768), f32),
        scratch_shapes=[pltpu.VMEM((rows, 1), f32), pltpu.VMEM((rows, 1), f32), pltpu.VMEM((rows, 128), f32)],
        compiler_params=_params(("parallel",)),
        name="win_attn",
    )(qn, gates, kv, d0, d1, d2)


def _diff_lambda(dl, lam_init):
    return (jnp.exp(jnp.sum(dl[0:1] * dl[1:2], axis=1, keepdims=True))
            - jnp.exp(jnp.sum(dl[2:3] * dl[3:4], axis=1, keepdims=True)) + lam_init)


def _diff_body(lam_init, qd_ref, kv_ref, d0_ref, d1_ref, far_ref, dl_ref, sg_ref, od_ref, m_ref, l_ref, acc_ref):
    i = pl.program_id(0)
    lane = lax.broadcasted_iota(jnp.int32, (TQ, 128), 1)
    hi = lane >= 64
    lam = _diff_lambda(dl_ref[...], lam_init)
    for h in range(DIFF_HEADS):
        _init_state(m_ref, l_ref, acc_ref)
        qc = qd_ref[:, h * 128:(h + 1) * 128]
        q = jnp.concatenate([jnp.where(lane < DIFF_QK, qc, jnp.zeros_like(qc)),
                             jnp.where(lane >= DIFF_QK, qc, jnp.zeros_like(qc))], axis=0)
        cb = far_ref[0:1, h:h + 1]

        def tile(j, bias):
            kt = kv_ref[pl.ds(pl.multiple_of(j * TQ, TQ), TQ), h * 128:(h + 1) * 128]
            s = _dot_nt(q, kt)
            if bias is None:
                _flash_step(s, cb, kt, m_ref, l_ref, acc_ref)
            else:
                b = bias[h]
                _flash_step(s + jnp.concatenate([b, b], axis=0), 0.0, kt, m_ref, l_ref, acc_ref)

        def far(j, c):
            tile(j, None)
            return c

        lax.fori_loop(0, jnp.maximum(i - 1, 0), far, 0)

        @pl.when(i >= 1)
        def _():
            tile(i - 1, d1_ref)

        tile(i, d0_ref)
        o = acc_ref[...] / jnp.maximum(l_ref[...], TINY)
        od = jnp.where(hi, o[:TQ] - lam * o[TQ:], 0.0)
        y = od * lax.rsqrt(jnp.sum(od * od, axis=1, keepdims=True) * (1.0 / DIFF_V) + EPS) * sg_ref[...]
        od_ref[:, h * 128:(h + 1) * 128] = y * (1.0 - lam_init)


def _diff_attn(qd, kv, d0, d1, far, dl, sg, lam_init):
    t = qd.shape[0]
    tile_spec = pl.BlockSpec((DIFF_HEADS, TQ, TQ), lambda i: (0, 0, 0))
    return pl.pallas_call(
        functools.partial(_diff_body, lam_init),
        grid=(t // TQ,),
        in_specs=[pl.BlockSpec((TQ, 512), lambda i: (i, 0)),
                  pl.BlockSpec((t, 512), lambda i: (0, 0)),
                  tile_spec, tile_spec,
                  pl.BlockSpec((8, 128), lambda i: (0, 0)),
                  pl.BlockSpec((4, DIFF_QK), lambda i: (0, 0)),
                  pl.BlockSpec((1, 128), lambda i: (0, 0))],
        out_specs=pl.BlockSpec((TQ, 512), lambda i: (i, 0)),
        out_shape=jax.ShapeDtypeStruct((t, 512), f32),
        scratch_shapes=[pltpu.VMEM((2 * TQ, 1), f32), pltpu.VMEM((2 * TQ, 1), f32), pltpu.VMEM((2 * TQ, 128), f32)],
        compiler_params=_params(("parallel",)),
        name="diff_attn",
    )(qd, kv, d0, d1, far, dl, sg)


LRU_TT = 256


def _lru_body(tt, x_ref, gi_ref, cbuf_ref, h0_ref, cw_ref, cb_ref, wa_ref, ba_ref, wx_ref, bx_ref, lam_ref,
              y_ref, hl_ref, cn_ref, xe_ref, a_ref, b_ref, h_ref):
    j = pl.program_id(1)

    @pl.when(j == 0)
    def _():
        xe_ref[0:8, :] = jnp.zeros((8, LRU_WIDTH), f32)
        xe_ref[5:8, :] = cbuf_ref[0]
        h_ref[0:1, :] = h0_ref[0]

    x = x_ref[0]
    xe_ref[8:8 + tt, :] = x
    xc = cb_ref[...] + sum(xe_ref[5 + k:5 + k + tt, :] * cw_ref[k:k + 1, :] for k in range(CONV_W))
    xcb = xc.astype(bf16)
    r = _sigmoid(_dot(xcb, wa_ref[...]) + ba_ref[...])
    ig = _sigmoid(_dot(xcb, wx_ref[...]) + bx_ref[...])
    lam = lam_ref[...]
    softplus = jnp.maximum(-lam, 0.0) + jnp.log(1.0 + jnp.exp(-jnp.abs(lam)))
    log_a = -LRU_C * r * softplus
    a = jnp.exp(log_a)
    a_ref[...] = a
    b_ref[...] = jnp.sqrt(1.0 - jnp.exp(2.0 * log_a)) * (ig * xc)

    def step(k, h):
        h = a_ref[pl.ds(k, 1), :] * h + b_ref[pl.ds(k, 1), :]
        b_ref[pl.ds(k, 1), :] = h
        return h

    h = lax.fori_loop(0, tt, step, h_ref[0:1, :])
    h_ref[0:1, :] = h
    y_ref[0] = b_ref[...] * _gelu(gi_ref[0])
    hl_ref[0] = h
    cn_ref[0] = xe_ref[5 + tt:8 + tt, :]
    xe_ref[5:8, :] = xe_ref[5 + tt:8 + tt, :]


def _rglru(x, gate_in, conv_buf, h0, cw, cb, wa, ba, wx, bx, lam):
    b, t, w = x.shape
    tt = min(LRU_TT, t)

    def bd(wblk):
        m = jnp.zeros((w, w), f32)
        for n in range(LRU_BLOCKS):
            m = m.at[n * LRU_BLOCK:(n + 1) * LRU_BLOCK, n * LRU_BLOCK:(n + 1) * LRU_BLOCK].set(wblk[n])
        return m.astype(bf16)

    vec = lambda: pl.BlockSpec((1, w), lambda bb, j: (0, 0))
    return pl.pallas_call(
        functools.partial(_lru_body, tt),
        grid=(b, t // tt),
        in_specs=[pl.BlockSpec((1, tt, w), lambda bb, j: (bb, j, 0)),
                  pl.BlockSpec((1, tt, w), lambda bb, j: (bb, j, 0)),
                  pl.BlockSpec((1, 3, w), lambda bb, j: (bb, 0, 0)),
                  pl.BlockSpec((1, 1, w), lambda bb, j: (bb, 0, 0)),
                  pl.BlockSpec((CONV_W, w), lambda bb, j: (0, 0)), vec(),
                  pl.BlockSpec((w, w), lambda bb, j: (0, 0)), vec(),
                  pl.BlockSpec((w, w), lambda bb, j: (0, 0)), vec(), vec()],
        out_specs=[pl.BlockSpec((1, tt, w), lambda bb, j: (bb, j, 0)),
                   pl.BlockSpec((1, 1, w), lambda bb, j: (bb, 0, 0)),
                   pl.BlockSpec((1, 3, w), lambda bb, j: (bb, 0, 0))],
        out_shape=[jax.ShapeDtypeStruct((b, t, w), f32), jax.ShapeDtypeStruct((b, 1, w), f32),
                   jax.ShapeDtypeStruct((b, 3, w), f32)],
        scratch_shapes=[pltpu.VMEM((tt + 8, w), f32), pltpu.VMEM((tt, w), f32), pltpu.VMEM((tt, w), f32),
                        pltpu.VMEM((8, w), f32)],
        compiler_params=_params(("parallel", "arbitrary")),
        name="rglru",
    )(x, gate_in, conv_buf, h0.reshape(b, 1, w), cw, cb.reshape(1, w), bd(wa), ba.reshape(1, w), bd(wx),
      bx.reshape(1, w), lam.reshape(1, w))


def _outproj_body(x_ref, gt_ref, oc_ref, os_ref, ow_ref, od_ref, yr_ref, w_ref, o_ref):
    on = oc_ref[...] + os_ref[...] + ow_ref[...]
    cat = jnp.concatenate([on.astype(bf16), od_ref[...].astype(bf16), yr_ref[...].astype(bf16)], axis=1)
    o_ref[...] = x_ref[...] + gt_ref[...] * _dot(cat, w_ref[...])


def _outproj(x, gt, oc, os_, ow, od, yr, w, tm):
    m = x.shape[0]
    row = lambda wd: pl.BlockSpec((tm, wd), lambda i: (i, 0))
    return pl.pallas_call(
        _outproj_body,
        grid=(m // tm,),
        in_specs=[row(D_MODEL), _rowspec(gt.shape[0], tm, D_MODEL), row(768), row(768), row(768), row(512),
                  row(384), pl.BlockSpec((N_OUT_IN, D_MODEL), lambda i: (0, 0))],
        out_specs=row(D_MODEL),
        out_shape=jax.ShapeDtypeStruct((m, D_MODEL), f32),
        compiler_params=_params(("parallel",)),
        name="outproj",
    )(x, gt, oc, os_, ow, od, yr, w)


def _ffn_body(n_exp, final, x_ref, g_ref, sc_ref, sh_ref, gt_ref, rw_ref, rb_ref, w1_ref, w3_ref, w2_ref, fg_ref,
              o_ref, h_ref, acc_ref, gate_ref):
    e, f = pl.program_id(1), pl.program_id(2)

    @pl.when((e == 0) & (f == 0))
    def _():
        h = _normmod(x_ref[...], g_ref[...], sc_ref[...], sh_ref[...])
        h_ref[...] = h.astype(bf16)
        acc_ref[...] = jnp.zeros_like(acc_ref)
        if n_exp > 1:
            logits = jnp.dot(h, rw_ref[...], preferred_element_type=f32, precision=lax.Precision.HIGHEST)
            lane = lax.broadcasted_iota(jnp.int32, logits.shape, 1)
            logits = jnp.where(lane < n_exp, logits + rb_ref[...], NEG_INF)
            v1 = jnp.max(logits, axis=1, keepdims=True)
            i1 = jnp.min(jnp.where(logits == v1, lane, 4096), axis=1, keepdims=True)
            rest = jnp.where(lane == i1, NEG_INF, logits)
            v2 = jnp.max(rest, axis=1, keepdims=True)
            i2 = jnp.min(jnp.where(rest == v2, lane, 4096), axis=1, keepdims=True)
            e2 = jnp.exp(v2 - v1)
            w_1 = 1.0 / (1.0 + e2)
            w_2 = e2 / (1.0 + e2)
            gate = jnp.where(lane == i1, w_1, 0.0) + jnp.where(lane == i2, w_2, 0.0)
            for k in range(n_exp):
                gate_ref[k] = gate[:, k:k + 1]

    hb = h_ref[...]
    hid = _silu(_dot(hb, w1_ref[0])) * _dot(hb, w3_ref[0])
    if n_exp > 1:
        hid = hid * gate_ref[e]
    acc_ref[...] += _dot(hid.astype(bf16), w2_ref[0])

    @pl.when((e == n_exp - 1) & (f == pl.num_programs(2) - 1))
    def _():
        y = x_ref[...] + gt_ref[...] * acc_ref[...]
        if final:
            y = y * lax.rsqrt(jnp.mean(y * y, axis=-1, keepdims=True) + EPS) * fg_ref[...]
        o_ref[...] = y


def _ffn(x, g, sc, sh, gt, rw, rb, w1, w3, w2, fg, final, tm, tf):
    m = x.shape[0]
    n_exp, _, ff = w1.shape
    vec = lambda: pl.BlockSpec((1, D_MODEL), lambda i, e, f: (0, 0))

    def rowspec(a):
        if a.shape[0] == 1:
            return vec()
        return pl.BlockSpec((tm, D_MODEL), lambda i, e, f: (i, 0))

    return pl.pallas_call(
        functools.partial(_ffn_body, n_exp, final),
        grid=(m // tm, n_exp, ff // tf),
        in_specs=[pl.BlockSpec((tm, D_MODEL), lambda i, e, f: (i, 0)), vec(), rowspec(sc), rowspec(sh), rowspec(gt),
                  pl.BlockSpec((D_MODEL, 128), lambda i, e, f: (0, 0)),
                  pl.BlockSpec((1, 128), lambda i, e, f: (0, 0)),
                  pl.BlockSpec((1, D_MODEL, tf), lambda i, e, f: (e, 0, f)),
                  pl.BlockSpec((1, D_MODEL, tf), lambda i, e, f: (e, 0, f)),
                  pl.BlockSpec((1, tf, D_MODEL), lambda i, e, f: (e, f, 0)),
                  vec()],
        out_specs=pl.BlockSpec((tm, D_MODEL), lambda i, e, f: (i, 0)),
        out_shape=jax.ShapeDtypeStruct((m, D_MODEL), f32),
        scratch_shapes=[pltpu.VMEM((tm, D_MODEL), bf16), pltpu.VMEM((tm, D_MODEL), f32),
                        pltpu.VMEM((N_EXPERTS, tm, 1), f32)],
        compiler_params=_params(("parallel", "arbitrary", "arbitrary")),
        name="ffn",
    )(x, g.reshape(1, -1), sc, sh, gt, rw, rb, w1, w3, w2, fg.reshape(1, -1))


def _t5_bucket(dist):
    n = jnp.maximum(dist, 0)
    exact = N_BUCKETS // 2
    nf = jnp.maximum(n, 1).astype(f32)
    large = exact + (jnp.log(nf / exact) / math.log(MAX_DISTANCE / exact) * (N_BUCKETS - exact)).astype(jnp.int32)
    return jnp.where(n < exact, n, jnp.minimum(large, N_BUCKETS - 1))


def _bias_tables(rel_bias):
    bd = rel_bias[_t5_bucket(jnp.arange(128, dtype=jnp.int32))]
    r = jnp.arange(TQ, dtype=jnp.int32)[:, None]
    c = jnp.arange(TQ, dtype=jnp.int32)[None, :]
    far = rel_bias[N_BUCKETS - 1]
    d0 = jnp.where((r - c >= 0)[..., None], bd[jnp.clip(r - c, 0, 127)], NEG_INF)
    d1 = bd[jnp.minimum(r - c + TQ, 127)]
    d2 = jnp.where((c > r)[..., None], far[None, None, :], NEG_INF)
    tr = lambda a: jnp.transpose(a, (2, 0, 1))
    bd_pad = jnp.zeros((16, 128), f32).at[:rel_bias.shape[1]].set(bd.T)
    far_pad = jnp.zeros((8, 128), f32).at[0, :rel_bias.shape[1]].set(far)
    return tr(d0), tr(d1), tr(d2), bd_pad, far_pad


def _m5(n_cmp):
    n = np.arange(n_cmp)[:, None]
    j = np.arange(256)[None, :]
    return jnp.asarray(((n >= 4 * j) & (n <= 4 * j + 4)).astype(np.float32), bf16)


def _jx_rmsnorm(x, g):
    xf = x.astype(f32)
    y = xf * lax.rsqrt(jnp.mean(xf * xf, axis=-1, keepdims=True) + EPS)
    return (y * g.astype(f32)).astype(x.dtype)


def _jx_masked_softmax(logits, mask):
    logits = jnp.where(mask, logits.astype(f32), NEG_INF)
    p = jnp.exp(logits - jnp.max(logits, axis=-1, keepdims=True)) * mask
    return p / jnp.maximum(jnp.sum(p, axis=-1, keepdims=True), TINY)


def _jx_split_proj(proj):
    sizes = (D_NSA, 3 * NSA_HEADS, 6 * NSA_KV_HEADS * HEAD_DIM, 2 * DIFF_HEADS * DIFF_QK,
             2 * DIFF_HEADS * DIFF_QK, D_DIFF, LRU_WIDTH, LRU_WIDTH)
    return jnp.split(proj, np.cumsum(sizes)[:-1].tolist(), axis=-1)


def _jx_compress(k, pe, w1, b1, w2):
    B, S = k.shape[:2]
    n_ch = S // CMP_STRIDE
    ch = k[:, :n_ch * CMP_STRIDE].reshape(B, n_ch, CMP_STRIDE, *k.shape[2:])
    first = ch[:, :-1] + pe[None, None, :CMP_STRIDE, None, :]
    second = ch[:, 1:] + pe[None, None, CMP_STRIDE:, None, :]
    hid = (jnp.einsum('bnlhd,lde->bnhe', first, w1[:CMP_STRIDE])
           + jnp.einsum('bnlhd,lde->bnhe', second, w1[CMP_STRIDE:]) + b1)
    return jnp.einsum('bnhe,ed->bnhd', jax.nn.gelu(hid), w2)


def _jx_rglru(x, gate_in, conv_buf, h0, conv_w, conv_b, wa, ba, wx, bx, lam):
    B, T, D = x.shape
    xe = jnp.concatenate([conv_buf, x], axis=1)
    xc = conv_b + sum(xe[:, j:j + T] * conv_w[j] for j in range(CONV_W))
    xr = xc.reshape(B, T, LRU_BLOCKS, LRU_BLOCK)
    r = jax.nn.sigmoid(jnp.einsum('btnd,nde->btne', xr, wa).reshape(B, T, D) + ba)
    i = jax.nn.sigmoid(jnp.einsum('btnd,nde->btne', xr, wx).reshape(B, T, D) + bx)
    log_a = -LRU_C * r.astype(f32) * jax.nn.softplus(-lam.astype(f32))
    a = jnp.exp(log_a)
    b = jnp.sqrt(-jnp.expm1(2.0 * log_a)) * (i * xc).astype(f32)
    b = b.at[:, 0].add(a[:, 0] * h0.astype(f32))

    def combine(left, right):
        a1, b1 = left
        a2, b2 = right
        return a1 * a2, a2 * b1 + b2

    _, h = lax.associative_scan(combine, (a, b), axis=1)
    y = (h * jax.nn.gelu(gate_in.astype(f32))).astype(x.dtype)
    return y, h[:, -1].astype(x.dtype), xe[:, -(CONV_W - 1):]


def _jx_mixer(h, past, p, layer_idx):
    B, T, _ = h.shape
    q_n, g_n, kv_n, q_d, k_d, v_d, x_r, g_r = _jx_split_proj(h @ p['w_in'])
    q_n = q_n.reshape(B, T, NSA_KV_HEADS, NSA_GROUP, HEAD_DIM)
    g_n = jax.nn.sigmoid(g_n.reshape(B, T, NSA_KV_HEADS, NSA_GROUP, 3))
    kv_n = kv_n.reshape(B, T, 3, 2, NSA_KV_HEADS, HEAD_DIM)
    new_cmp, new_sel, new_win = kv_n[:, :, 0], kv_n[:, :, 1], kv_n[:, :, 2]
    q_d = q_d.reshape(B, T, DIFF_HEADS, 2, DIFF_QK)
    new_diff = jnp.concatenate([k_d.reshape(B, T, DIFF_HEADS, 2 * DIFF_QK),
                                v_d.reshape(B, T, DIFF_HEADS, DIFF_V)], axis=-1)
    cmp_full = jnp.concatenate([past['cmp'], new_cmp], axis=1)
    sel_full = jnp.concatenate([past['sel'], new_sel], axis=1)
    diff_full = jnp.concatenate([past['diff'], new_diff], axis=1)
    S = cmp_full.shape[1]
    pos0 = S - T
    kc = _jx_compress(cmp_full[:, :, 0], p['cmp_pe'][0], p['cmp_w1'][0], p['cmp_b1'][0], p['cmp_w2'][0])
    vc = _jx_compress(cmp_full[:, :, 1], p['cmp_pe'][1], p['cmp_w1'][1], p['cmp_b1'][1], p['cmp_w2'][1])
    n_cmp = kc.shape[1]
    c_end = jnp.arange(n_cmp, dtype=jnp.int32) * CMP_STRIDE + (CMP_LEN - 1)
    n_sel = -(-S // SEL_BLOCK)
    k_sel = min(SEL_TOPK, n_sel)
    sel_b = jnp.pad(sel_full, ((0, 0), (0, n_sel * SEL_BLOCK - S), (0, 0), (0, 0), (0, 0)))
    sel_b = sel_b.reshape(B, n_sel, SEL_BLOCK, 2, NSA_KV_HEADS, HEAD_DIM).transpose(3, 0, 4, 1, 2, 5)
    ks_b, vs_b = sel_b[0], sel_b[1]
    win_full = jnp.concatenate([past['win'], new_win], axis=1)
    n_keep = min(WINDOW, win_full.shape[1])
    win_pad = jnp.pad(win_full, ((0, 0), (WINDOW - past['win'].shape[1], 0), (0, 0), (0, 0), (0, 0)))
    kw, vw = win_pad[:, :, 0], win_pad[:, :, 1]
    kd1 = diff_full[..., :DIFF_QK]
    kd2 = diff_full[..., DIFF_QK:2 * DIFF_QK]
    vd = diff_full[..., 2 * DIFF_QK:]
    tab_nsa = p['rel_bias'][:, :NSA_HEADS].reshape(N_BUCKETS, NSA_KV_HEADS, NSA_GROUP)
    tab_hg = tab_nsa.transpose(1, 0, 2)
    tab_diff = p['rel_bias'][:, NSA_HEADS:]
    lam_init = 0.8 - 0.6 * math.exp(-0.3 * layer_idx)
    dl = p['diff_lambda'].astype(f32)
    lam = jnp.exp(jnp.sum(dl[0] * dl[1])) - jnp.exp(jnp.sum(dl[2] * dl[3])) + lam_init
    QB = T
    bidx = jnp.arange(B)[:, None, None, None]
    hidx = jnp.arange(NSA_KV_HEADS)[None, None, :, None]
    key_pos = jnp.arange(S, dtype=jnp.int32)
    qn, gn, q1, q2, b0 = q_n, g_n, q_d[:, :, :, 0], q_d[:, :, :, 1], 0
    qp = pos0 + b0 + jnp.arange(QB, dtype=jnp.int32)
    dist_c = qp[:, None] - c_end[None, :]
    s_c = (jnp.einsum('bqhgd,bnhd->bqhgn', qn, kc) * ATT_SCALE
           + tab_nsa[_t5_bucket(dist_c)].transpose(0, 2, 3, 1)[None])
    p_c = _jx_masked_softmax(s_c, (dist_c >= 0)[None, :, None, None, :])
    o_c = jnp.einsum('bqhgn,bnhd->bqhgd', p_c, vc)
    imp = jnp.sum(p_c, axis=3)
    imp = jnp.pad(imp, ((0, 0), (0, 0), (0, 0), (1, SEL_RATIO * (n_sel + 1) - 1 - n_cmp)))
    imp = imp.reshape(B, QB, NSA_KV_HEADS, n_sel + 1, SEL_RATIO)
    p_slc = jnp.sum(imp[..., :n_sel, :], axis=-1) + imp[..., 1:, 0]
    blk = jnp.arange(n_sel, dtype=jnp.int32)
    forced = (blk[None] == (qp // SEL_BLOCK)[:, None]) | (blk[None] == 0)
    allowed = blk[None] * SEL_BLOCK <= qp[:, None]
    score = jnp.where(forced[None, :, None, :], FORCE_SCORE,
                      jnp.where(allowed[None, :, None, :], p_slc, -FORCE_SCORE))
    _, idx = lax.top_k(score, k_sel)
    L = k_sel * SEL_BLOCK
    ks = ks_b[bidx, hidx, idx].reshape(B, QB, NSA_KV_HEADS, L, HEAD_DIM)
    vs = vs_b[bidx, hidx, idx].reshape(B, QB, NSA_KV_HEADS, L, HEAD_DIM)
    tok = (idx[..., None] * SEL_BLOCK + jnp.arange(SEL_BLOCK, dtype=jnp.int32)).reshape(B, QB, NSA_KV_HEADS, L)
    dist_s = qp[None, :, None, None] - tok
    bias_s = tab_hg[hidx, _t5_bucket(dist_s)]
    s_s = jnp.einsum('bqhgd,bqhld->bqhgl', qn, ks) * ATT_SCALE + jnp.swapaxes(bias_s, -1, -2)
    p_s = _jx_masked_softmax(s_s, (dist_s >= 0)[:, :, :, None, :])
    o_s = jnp.einsum('bqhgl,bqhld->bqhgd', p_s, vs)
    kw_blk = lax.dynamic_slice_in_dim(kw, b0, WINDOW + QB, axis=1)
    vw_blk = lax.dynamic_slice_in_dim(vw, b0, WINDOW + QB, axis=1)
    kpos = pos0 - WINDOW + b0 + jnp.arange(WINDOW + QB, dtype=jnp.int32)
    dist_w = qp[:, None] - kpos[None, :]
    mask_w = (dist_w >= 0) & (dist_w < WINDOW) & (kpos >= 0)[None, :]
    s_w = (jnp.einsum('bqhgd,bkhd->bqhgk', qn, kw_blk) * ATT_SCALE
           + tab_nsa[_t5_bucket(dist_w)].transpose(0, 2, 3, 1)[None])
    p_w = _jx_masked_softmax(s_w, mask_w[None, :, None, None, :])
    o_w = jnp.einsum('bqhgk,bkhd->bqhgd', p_w, vw_blk)
    o_n = gn[..., 0:1] * o_c + gn[..., 1:2] * o_s + gn[..., 2:3] * o_w
    dist_d = qp[:, None] - key_pos[None, :]
    bias_d = tab_diff[_t5_bucket(dist_d)].transpose(2, 0, 1)[None]
    mask_d = (dist_d >= 0)[None, None]
    a1 = _jx_masked_softmax(jnp.einsum('bqhd,bkhd->bhqk', q1, kd1) * DIFF_SCALE + bias_d, mask_d)
    a2 = _jx_masked_softmax(jnp.einsum('bqhd,bkhd->bhqk', q2, kd2) * DIFF_SCALE + bias_d, mask_d)
    o_d = jnp.einsum('bhqk,bkhd->bqhd', a1 - lam * a2, vd)
    o_n = o_n.reshape(B, T, D_NSA)
    o_d = (_jx_rmsnorm(o_d, p['diff_subln_g']) * (1.0 - lam_init)).reshape(B, T, D_DIFF)
    y_r, h_last, conv_new = _jx_rglru(x_r, g_r, past['conv'], past['h'], p['lru_conv_w'], p['lru_conv_b'],
                                      p['lru_wa'], p['lru_ba'], p['lru_wx'], p['lru_bx'], p['lru_lambda'])
    mix = jnp.concatenate([o_n.astype(h.dtype), o_d.astype(h.dtype), y_r], axis=-1) @ p['w_out']
    return mix, (new_cmp, new_sel, new_diff, win_full[:, -n_keep:], h_last, conv_new)


def _jx_gather_pages(pool, l, page_table):
    flat = pool[l].reshape(pool.shape[1], -1)
    g = jnp.take(flat, page_table.reshape(-1), axis=0)
    return g.reshape(page_table.shape[0], page_table.shape[1] * pool.shape[2], *pool.shape[3:])


def _split6(mod):
    return [mod[:, k * D_MODEL:(k + 1) * D_MODEL] for k in range(6)]


def _prompt_layer(x, mod, lw, tabs, l, final_g, last):
    t = x.shape[0]
    sh1, sc1, gt1, sh2, sc2, gt2 = _split6(mod)
    (d0n, d1n, d2n, d0d, d1d, bd_nsa, far_n, far_d, m5) = tabs
    (qn, qd, gates, cmp, sel, win, diff, diffb, xr, gr, selaug, winb) = _proj(
        x, lw['norm_mix_g'], sc1, sh1, lw['w_aug'], 256)
    n_pages = t // PAGE
    ident = jnp.arange(n_pages, dtype=jnp.int32).reshape(1, n_pages)
    slab = _compress(cmp.reshape(n_pages, 8, 4096), ident, *lw['cmp'])[0]
    oc, mneg = _cmp_attn(qn, gates, slab, bd_nsa, m5)
    os_ = _sel_attn(qn, mneg, gates, selaug, d0n, d1n, far_n)
    ow = _win_attn(qn, gates, winb, d0n, d1n, d2n)
    lam_init = 0.8 - 0.6 * math.exp(-0.3 * l)
    od = _diff_attn(qd, diffb, d0d, d1d, far_d, lw['diff_lambda'], lw['subln'], lam_init)
    zeros_c = jnp.zeros((1, CONV_W - 1, LRU_WIDTH), f32)
    zeros_h = jnp.zeros((1, LRU_WIDTH), f32)
    yr, hl, cn = _rglru(xr[None], gr[None], zeros_c, zeros_h, *lw['lru'])
    x = _outproj(x, gt1, oc, os_, ow, od, yr[0], lw['w_out'], 512)
    x = _ffn(x, lw['norm_ffn_g'], sc2, sh2, gt2, *lw['ffn'], final_g, last, 512, 1408)
    n_keep = min(WINDOW, t)
    state = (cmp.reshape(1, t, 2, NSA_KV_HEADS, HEAD_DIM), sel.reshape(1, t, 2, NSA_KV_HEADS, HEAD_DIM),
             diff.reshape(1, t, DIFF_HEADS, DIFF_ROW), win[t - n_keep:].reshape(1, n_keep, 2, NSA_KV_HEADS, HEAD_DIM),
             hl.reshape(1, LRU_WIDTH), cn)
    return x, state


def kernel(x_prompt, x_sample, c_prompt, c_sample, cache_nsa_cmp, cache_nsa_sel, cache_diff, cache_nsa_win, state_lru_h, state_lru_conv, page_table, rel_bias, norm_mix_g, norm_ffn_g, final_norm_g, w_ada, b_ada, w_in, cmp_pe, cmp_w1, cmp_b1, cmp_w2, diff_lambda, diff_subln_g, lru_conv_w, lru_conv_b, lru_wa, lru_ba, lru_wx, lru_bx, lru_lambda, w_out, ffn_w1, ffn_w3, ffn_w2, router_w, router_b, moe_w1, moe_w3, moe_w2):
    depth = w_in.shape[0]
    t = x_prompt.shape[1]
    nb = c_sample.shape[0]
    cols, rows = _proj_cols(), _outproj_rows()
    d0, d1, d2, bd, far = _bias_tables(rel_bias)
    nh = NSA_HEADS
    far_d = jnp.zeros((8, 128), f32).at[0, :DIFF_HEADS].set(far[0, nh:nh + DIFF_HEADS])
    tabs = (d0[:nh], d1[:nh], d2[:nh], d0[nh:], d1[nh:], bd, far, far_d, _m5(t // CMP_STRIDE))

    c_all = jnp.concatenate([c_prompt, c_sample, jnp.zeros((7, D_MODEL), f32)], axis=0)
    layers, jx_layers = [], []
    for l in range(depth):
        j = l // 2
        lw = {'norm_mix_g': norm_mix_g[l], 'norm_ffn_g': norm_ffn_g[l],
              'w_aug': _take_cols(w_in[l], cols).astype(bf16),
              'cmp': _compress_weights(cmp_pe[l], cmp_w1[l], cmp_b1[l], cmp_w2[l]),
              'diff_lambda': diff_lambda[l],
              'subln': jnp.zeros((1, 128), f32).at[0, 64:].set(diff_subln_g[l]),
              'lru': (lru_conv_w[l], lru_conv_b[l], lru_wa[l], lru_ba[l], lru_wx[l], lru_bx[l], lru_lambda[l]),
              'w_out': _take_rows(w_out[l], rows).astype(bf16)}
        if l % 2 == 0:
            lw['ffn'] = (jnp.zeros((D_MODEL, 128), f32), jnp.zeros((1, 128), f32),
                         ffn_w1[j][None].astype(bf16), ffn_w3[j][None].astype(bf16), ffn_w2[j][None].astype(bf16))
        else:
            rw = jnp.zeros((D_MODEL, 128), f32).at[:, :N_EXPERTS].set(router_w[j])
            rb = jnp.zeros((1, 128), f32).at[0, :N_EXPERTS].set(router_b[j])
            lw['ffn'] = (rw, rb, moe_w1[j].astype(bf16), moe_w3[j].astype(bf16), moe_w2[j].astype(bf16))
        layers.append(lw)
        p = {'rel_bias': rel_bias, 'norm_mix_g': norm_mix_g[l], 'norm_ffn_g': norm_ffn_g[l],
             'w_in': w_in[l], 'cmp_pe': cmp_pe[l], 'cmp_w1': cmp_w1[l], 'cmp_b1': cmp_b1[l], 'cmp_w2': cmp_w2[l],
             'diff_lambda': diff_lambda[l], 'diff_subln_g': diff_subln_g[l], 'lru_conv_w': lru_conv_w[l],
             'lru_conv_b': lru_conv_b[l], 'lru_wa': lru_wa[l], 'lru_ba': lru_ba[l], 'lru_wx': lru_wx[l],
             'lru_bx': lru_bx[l], 'lru_lambda': lru_lambda[l], 'w_out': w_out[l]}
        jx_layers.append(p)

    mods = [_ada(c_all, w_ada[l], b_ada[l]) for l in range(depth)]

    x = x_prompt[0]
    p_states = []
    for l in range(depth):
        x, st = _prompt_layer(x, mods[l][0:1], layers[l], tabs, l, final_norm_g, l == depth - 1)
        p_states.append(st)
    y_prompt = x[None]

    xs = x_sample
    s_states = []
    dec = x_sample.shape[1]
    for l in range(depth):
        lw, p = layers[l], jx_layers[l]
        mod = mods[l][1:1 + nb]
        sh1, sc1, gt1, sh2, sc2, gt2 = [m[:, None, :] for m in _split6(mod)]
        past = {'cmp': _jx_gather_pages(cache_nsa_cmp, l, page_table),
                'sel': _jx_gather_pages(cache_nsa_sel, l, page_table),
                'diff': _jx_gather_pages(cache_diff, l, page_table),
                'win': cache_nsa_win[l], 'h': state_lru_h[l], 'conv': state_lru_conv[l]}
        h = _jx_rmsnorm(xs, p['norm_mix_g']) * (1.0 + sc1) + sh1
        mix, st = _jx_mixer(h, past, p, l)
        xs = xs + gt1 * mix
        rep = lambda a: jnp.repeat(a[:, 0], dec, axis=0)
        x2 = _ffn(xs.reshape(nb * dec, D_MODEL), lw['norm_ffn_g'], rep(sc2), rep(sh2), rep(gt2), *lw['ffn'],
                  final_norm_g, l == depth - 1, nb * dec, 1408)
        xs = x2.reshape(nb, dec, D_MODEL)
        s_states.append(st)
    y_sample = xs

    stack = lambda sts, k: jnp.stack([s[k] for s in sts], axis=0)
    return (y_prompt, y_sample, *[stack(p_states, k) for k in range(6)], *[stack(s_states, k) for k in range(6)])
```

```python
import functools
import math

import numpy as np
import jax
import jax.numpy as jnp
from jax import lax
from jax.experimental import pallas as pl
from jax.experimental.pallas import tpu as pltpu

f32 = jnp.float32
bf16 = jnp.bfloat16

D_MODEL = 1024
HEAD_DIM = 64
NSA_HEADS = 6
NSA_KV_HEADS = 2
NSA_GROUP = 3
CMP_STRIDE = 16
CMP_LEN = 32
CMP_HIDDEN = 128
SEL_BLOCK = 64
SEL_RATIO = 4
SEL_TOPK = 16
WINDOW = 512
DIFF_HEADS = 4
DIFF_QK = 32
DIFF_V = 64
DIFF_ROW = 128
LRU_WIDTH = 384
LRU_BLOCKS = 6
LRU_BLOCK = 64
CONV_W = 4
LRU_C = 8.0
D_NSA = 384
D_DIFF = 256
N_BUCKETS = 32
MAX_DISTANCE = 128
D_FF = 2816
N_EXPERTS = 8
D_FF_EXPERT = 1408
EPS = 1e-6
NEG_INF = -1e30
TINY = 1e-30
FORCE_SCORE = 1e4
ATT_SCALE = HEAD_DIM ** -0.5
DIFF_SCALE = DIFF_QK ** -0.5
PAGE = 128
LOG2E = 1.4426950408889634

TQ = 256
N_BLK_LANES = 256
M_INIT = -1e20
VMEM_LIMIT = 56 * 1024 * 1024
DEC_PAGES = 16
DEC_KEYS = DEC_PAGES * PAGE

(O_QN, O_QD, O_GATE, O_CMP, O_SEL, O_WIN, O_DIFF, O_XR, O_GR, O_KAUG, O_VSEL, O_KWIN, O_VWIN, O_KDIFF, O_VDIFF,
 N_PROJ) = (0, 768, 1280, 1408, 1664, 1920, 2176, 2688, 3072, 3456, 3712, 3968, 4096, 4352, 4608, 5120)
S_QN, S_GN, S_CMP, S_SEL, S_WIN, S_QD, S_KD, S_VD, S_XR, S_GR, D_IN = (
    0, 384, 402, 658, 914, 1170, 1426, 1682, 1938, 2322, 2706)
N_OUT_IN = 768 + 512 + 384


def _proj_cols():
    c, one = [], []

    def seg(cols, ones=None):
        c.extend(cols)
        one.extend(ones if ones is not None else [0.0] * len(cols))

    for h in range(NSA_HEADS):
        hk = h // NSA_GROUP
        seg([-1] * (hk * 64) + [S_QN + h * 64 + d for d in range(64)] + [-1] * (64 - hk * 64))
    for h in range(DIFF_HEADS):
        o = (h % 2) * 64
        seg([-1] * o + [S_QD + h * 64 + d for d in range(64)] + [-1] * (64 - o))
    seg([S_GN + i for i in range(18)] + [-1] * 110)
    seg(list(range(S_CMP, S_CMP + 256)) + list(range(S_SEL, S_SEL + 256)) + list(range(S_WIN, S_WIN + 256)))
    for h in range(DIFF_HEADS):
        seg([S_KD + h * 64 + d for d in range(64)] + [S_VD + h * 64 + d for d in range(64)])
    seg(list(range(S_XR, S_XR + 384)) + list(range(S_GR, S_GR + 384)))
    seg([-1] * 128 + list(range(S_SEL, S_SEL + 128)))
    for hk in range(NSA_KV_HEADS):
        seg([S_SEL + 128 + hk * 64 + d for d in range(64)] + [-1] * 64, [0.0] * 64 + [1.0] * 64)
    seg(list(range(S_WIN, S_WIN + 128)))
    for hk in range(NSA_KV_HEADS):
        seg([S_WIN + 128 + hk * 64 + d for d in range(64)] + [-1] * 64, [0.0] * 64 + [1.0] * 64)
    seg(list(range(S_KD, S_KD + 256)))
    for h in range(DIFF_HEADS):
        seg([S_VD + h * 64 + d for d in range(64)] + [-1] * 64, [0.0] * 64 + [1.0] * 64)
    assert len(c) == N_PROJ
    return np.asarray(c, np.int32), np.asarray(one, np.float32).reshape(1, N_PROJ)


def _outproj_rows():
    r = []
    for h in range(NSA_HEADS):
        r += [h * 64 + d for d in range(64)] + [-1] * 64
    for h in range(DIFF_HEADS):
        r += [D_NSA + h * 64 + d for d in range(64)] + [-1] * 64
    r += list(range(D_NSA + D_DIFF, 1024))
    assert len(r) == N_OUT_IN
    return np.asarray(r, np.int32)


def _take_cols(w, cols):
    return jnp.where(cols[None, :] >= 0, jnp.take(w, np.maximum(cols, 0), axis=1), 0.0)


def _take_rows(w, rows):
    return jnp.where(rows[:, None] >= 0, jnp.take(w, np.maximum(rows, 0), axis=0), 0.0)


def _params(sem):
    return pltpu.CompilerParams(dimension_semantics=sem, vmem_limit_bytes=VMEM_LIMIT)


def _dot(a, b):
    return jnp.dot(a, b, preferred_element_type=f32)


def _dot_nt(a, b):
    return lax.dot_general(a, b, (((1,), (1,)), ((), ())), preferred_element_type=f32)


def _gelu(x):
    return 0.5 * x * (1.0 + jnp.tanh(math.sqrt(2.0 / math.pi) * (x + 0.044715 * (x * x * x))))


def _sigmoid(x):
    return 1.0 / (1.0 + jnp.exp(-x))


def _silu(x):
    return x * _sigmoid(x)


def _normmod(x, g, sc, sh):
    h = x * lax.rsqrt(jnp.mean(x * x, axis=-1, keepdims=True) + EPS) * g
    return h * (1.0 + sc) + sh


def _rowspec(rows, tm, width):
    if rows == 1:
        return pl.BlockSpec((1, width), lambda i: (0, 0))
    return pl.BlockSpec((tm, width), lambda i: (i, 0))


def _resident(shape):
    return pl.BlockSpec(shape, lambda i: (0,) * len(shape), pipeline_mode=pl.Buffered(1))


def _lane(rows):
    return lax.broadcasted_iota(jnp.int32, (rows, 128), 1)


def _ada_body(c_ref, w_ref, b_ref, o_ref):
    c = c_ref[...]
    o_ref[...] = _dot(_silu(c).astype(bf16), w_ref[...].astype(bf16)) + b_ref[...]


def _ada(c, w, b):
    m, n = c.shape[0], w.shape[1]
    tn = 1536
    return pl.pallas_call(
        _ada_body,
        grid=(n // tn,),
        in_specs=[pl.BlockSpec((m, D_MODEL), lambda j: (0, 0)),
                  pl.BlockSpec((D_MODEL, tn), lambda j: (0, j)),
                  pl.BlockSpec((1, tn), lambda j: (0, j))],
        out_specs=pl.BlockSpec((m, tn), lambda j: (0, j)),
        out_shape=jax.ShapeDtypeStruct((m, n), f32),
        compiler_params=_params(("parallel",)),
        name="ada",
    )(c, w, b.reshape(1, n))


PROJ_OUTS = [(768, bf16), (512, bf16), (128, f32), (256, f32), (256, f32), (256, f32), (512, f32), (384, f32),
             (384, f32), (256, bf16), (256, bf16), (128, bf16), (256, bf16), (256, bf16), (512, bf16)]


def _proj_body(tm, x_ref, g_ref, sc_ref, sh_ref, w_ref, one_ref, qn_ref, qd_ref, gate_ref, cmp_ref, sel_ref,
               win_ref, diff_ref, xr_ref, gr_ref, kaug_ref, vsel_ref, kwin_ref, vwin_ref, kdiff_ref, vdiff_ref):
    h = _normmod(x_ref[...], g_ref[...], sc_ref[...], sh_ref[...])
    pr = _dot(h.astype(bf16), w_ref[...]) + one_ref[...]
    qn_ref[...] = (pr[:, O_QN:O_QD] * (ATT_SCALE * LOG2E)).astype(bf16)
    qd_ref[...] = (pr[:, O_QD:O_GATE] * (DIFF_SCALE * LOG2E)).astype(bf16)
    gate_ref[...] = _sigmoid(pr[:, O_GATE:O_CMP])
    cmp_ref[...] = pr[:, O_CMP:O_SEL]
    sel_ref[...] = pr[:, O_SEL:O_WIN]
    win_ref[...] = pr[:, O_WIN:O_DIFF]
    diff_ref[...] = pr[:, O_DIFF:O_XR]
    xr_ref[...] = pr[:, O_XR:O_GR]
    gr_ref[...] = pr[:, O_GR:O_KAUG]
    t = pl.program_id(0) * tm + lax.broadcasted_iota(jnp.int32, (tm, 128), 0)
    onehot = (_lane(tm) == (t // SEL_BLOCK) % 128).astype(f32)
    kaug_ref[:, 0:128] = onehot.astype(bf16)
    kaug_ref[:, 128:256] = pr[:, O_KAUG + 128:O_VSEL].astype(bf16)
    vsel_ref[...] = pr[:, O_VSEL:O_KWIN].astype(bf16)
    kwin_ref[...] = pr[:, O_KWIN:O_VWIN].astype(bf16)
    vwin_ref[...] = pr[:, O_VWIN:O_KDIFF].astype(bf16)
    kdiff_ref[...] = pr[:, O_KDIFF:O_VDIFF].astype(bf16)
    vdiff_ref[...] = pr[:, O_VDIFF:N_PROJ].astype(bf16)


def _proj(x, g, sc, sh, w, onerow, tm):
    m = x.shape[0]
    return pl.pallas_call(
        functools.partial(_proj_body, tm),
        grid=(m // tm,),
        in_specs=[pl.BlockSpec((tm, D_MODEL), lambda i: (i, 0)),
                  pl.BlockSpec((1, D_MODEL), lambda i: (0, 0)),
                  _rowspec(sc.shape[0], tm, D_MODEL), _rowspec(sh.shape[0], tm, D_MODEL),
                  pl.BlockSpec((D_MODEL, N_PROJ), lambda i: (0, 0)),
                  pl.BlockSpec((1, N_PROJ), lambda i: (0, 0))],
        out_specs=[pl.BlockSpec((tm, wd), lambda i: (i, 0)) for wd, _ in PROJ_OUTS],
        out_shape=[jax.ShapeDtypeStruct((m, wd), dt) for wd, dt in PROJ_OUTS],
        compiler_params=_params(("parallel",)),
        name="proj",
    )(x, g.reshape(1, -1), sc, sh, w, onerow)


CMP_PAGES = 16
CMP_ROWS = CMP_PAGES * 8
CMP_OUT = 384


def _compress_body(*refs):
    pages = refs[1:1 + CMP_PAGES]
    wc_ref, pe_ref, b1_ref, w2_ref, o_ref, carry_ref = refs[1 + CMP_PAGES:]
    g = pl.program_id(1)

    @pl.when(g == 0)
    def _():
        carry_ref[...] = jnp.zeros_like(carry_ref)

    x = jnp.concatenate([p[0, 0] for p in pages], axis=0).astype(bf16)
    a = _dot(x, wc_ref[...])
    pe = _dot(pe_ref[...].astype(bf16), wc_ref[...])
    const = pe[0:1, :512] + pe[1:2, 512:] + b1_ref[...]
    first, second = a[:, :512], a[:, 512:]
    rid = lax.broadcasted_iota(jnp.int32, (CMP_ROWS, 512), 0)
    prev_first = jnp.where(rid == 0, carry_ref[0:1, :], pltpu.roll(first, 1, 0))
    carry_ref[0:1, :] = first[CMP_ROWS - 1:CMP_ROWS, :]
    hid = _gelu(prev_first + second + const)
    o_ref[0] = _dot(hid.astype(bf16), w2_ref[...]).astype(bf16)


def _compress(pool, table, wc, pe2, b1, w2bd):
    b, n_pages = table.shape
    steps = n_pages // CMP_PAGES

    def page_spec(i):
        return pl.BlockSpec((1, 1, 8, 4096), lambda bb, g, tab: (tab[bb, g * CMP_PAGES + i], 0, 0, 0))

    gs = pltpu.PrefetchScalarGridSpec(
        num_scalar_prefetch=1,
        grid=(b, steps),
        in_specs=[page_spec(i) for i in range(CMP_PAGES)] + [
            pl.BlockSpec((4096, 1024), lambda bb, g, tab: (0, 0)),
            pl.BlockSpec((8, 4096), lambda bb, g, tab: (0, 0)),
            pl.BlockSpec((1, 512), lambda bb, g, tab: (0, 0)),
            pl.BlockSpec((512, CMP_OUT), lambda bb, g, tab: (0, 0))],
        out_specs=pl.BlockSpec((1, CMP_ROWS, CMP_OUT), lambda bb, g, tab: (bb, g, 0)),
        scratch_shapes=[pltpu.VMEM((8, 512), f32)],
    )
    pool4 = pool.reshape(pool.shape[0], 1, 8, 4096)
    return pl.pallas_call(
        _compress_body,
        grid_spec=gs,
        out_shape=jax.ShapeDtypeStruct((b, n_pages * 8, CMP_OUT), bf16),
        compiler_params=_params(("parallel", "arbitrary")),
        name="compress",
    )(table, *([pool4] * CMP_PAGES), wc, pe2, b1, w2bd)


def _compress_weights(cmp_pe, cmp_w1, cmp_b1, cmp_w2):
    w1 = cmp_w1.reshape(2, 2, CMP_STRIDE, HEAD_DIM, CMP_HIDDEN)
    eye4 = jnp.eye(4, dtype=f32)
    w1s = jnp.stack([w1[0], w1[0], w1[1], w1[1]], axis=0)
    wc = jnp.einsum('sflde,st->lsdfte', w1s, eye4).reshape(4096, 1024)
    pe = cmp_pe.reshape(2, 2, CMP_STRIDE, HEAD_DIM)
    pes = jnp.stack([pe[0], pe[0], pe[1], pe[1]], axis=0)
    pe2 = jnp.transpose(pes, (1, 2, 0, 3)).reshape(2, 4096)
    pe2 = jnp.concatenate([pe2, jnp.zeros((6, 4096), f32)], axis=0)
    b1 = jnp.stack([cmp_b1[0], cmp_b1[0], cmp_b1[1], cmp_b1[1]], axis=0).reshape(1, 512)
    w2bd = jnp.zeros((512, CMP_OUT), f32)
    for s, lane0 in ((0, 0), (1, 64), (2, 128), (3, 256)):
        w2bd = w2bd.at[s * 128:(s + 1) * 128, lane0:lane0 + 64].set(cmp_w2[s // 2])
    return wc.astype(bf16), pe2, b1, w2bd.astype(bf16)


def _lane_tile(x, reps):
    return x if reps == 1 else jnp.concatenate([x] * reps, axis=1)


def _flash_tile(s, cbm, vt, m_ref, acc_ref, idx, l_ref=None):
    m_prev = m_ref[idx]
    m_new = jnp.maximum(m_prev, jnp.max(s, axis=1, keepdims=True) + cbm)
    alpha = jnp.exp2(m_prev - m_new)
    p = jnp.exp2(s - _lane_tile(m_new - cbm, s.shape[1] // 128))
    if l_ref is not None:
        l_ref[idx] = alpha * l_ref[idx] + jnp.sum(p, axis=1, keepdims=True)
    reps = acc_ref.shape[-1] // 128
    a = _lane_tile(alpha, reps)
    acc_ref[idx] = a * acc_ref[idx] + _dot(p.astype(bf16), vt)
    m_ref[idx] = m_new


def _init_state(m_ref, acc_ref, l_ref=None):
    m_ref[...] = jnp.full(m_ref.shape, M_INIT, f32)
    acc_ref[...] = jnp.zeros(acc_ref.shape, f32)
    if l_ref is not None:
        l_ref[...] = jnp.zeros(l_ref.shape, f32)


def _norm_low(acc):
    l = pltpu.roll(acc, 64, 1)
    return jnp.where(_lane(acc.shape[0]) < 64, acc / jnp.maximum(l, TINY), 0.0)


def _cmp_body(tq, pos0, n_pick, qn_ref, gate_ref, slab_ref, bd_ref, m5_ref, oc_ref, mneg_ref):
    i = pl.program_id(1)
    n_cmp = slab_ref.shape[1]
    t = pos0 + i * tq + lax.broadcasted_iota(jnp.int32, (tq, n_cmp), 0)
    n = lax.broadcasted_iota(jnp.int32, (tq, n_cmp), 1)
    dist = t - (CMP_STRIDE * n + CMP_STRIDE - 1)
    valid = (dist >= 0) & (n >= 1)
    idx = jnp.clip(dist, 0, 127)
    low = _lane(tq) < 64
    blk = lax.broadcasted_iota(jnp.int32, (tq, N_BLK_LANES), 1)
    tb = pos0 + i * tq + lax.broadcasted_iota(jnp.int32, (tq, N_BLK_LANES), 0)
    forced = (blk == tb // SEL_BLOCK) | (blk == 0)
    allowed = blk * SEL_BLOCK <= tb
    small = tq % 16 != 0
    kc = slab_ref[0, :, 0:128]
    kc = kc.astype(f32) if small else kc
    for hk in range(NSA_KV_HEADS):
        vc = slab_ref[0, :, 128 + hk * 128:256 + hk * 128]
        imp = jnp.zeros((tq, n_cmp), f32)
        for g in range(NSA_GROUP):
            h = hk * NSA_GROUP + g
            q = qn_ref[0, :, h * 128:(h + 1) * 128]
            s = _dot_nt(q.astype(f32) if small else q, kc)
            tab = jnp.broadcast_to(bd_ref[h:h + 1, :], (tq, 128))
            bias = jnp.concatenate(
                [jnp.take_along_axis(tab, idx[:, c * 128:(c + 1) * 128], axis=1) for c in range(n_cmp // 128)],
                axis=1)
            s = jnp.where(valid, s + bias, NEG_INF)
            p = jnp.exp2(s - jnp.max(s, axis=1, keepdims=True)) * valid.astype(f32)
            p = p / jnp.maximum(jnp.sum(p, axis=1, keepdims=True), TINY)
            imp = imp + p
            o = _dot(p, vc.astype(f32)) if small else _dot(p.astype(bf16), vc)
            gate = gate_ref[0, :, h * 3:h * 3 + 1]
            oc_ref[0, :, h * 128:(h + 1) * 128] = jnp.where(low, o * gate, 0.0)
        imp_hi = imp.astype(bf16)
        imp_lo = (imp - imp_hi.astype(f32)).astype(bf16)
        if small:
            m5 = m5_ref[...].astype(f32)
            p_slc = _dot(imp_hi.astype(f32), m5) + _dot(imp_lo.astype(f32), m5)
        else:
            p_slc = _dot(imp_hi, m5_ref[...]) + _dot(imp_lo, m5_ref[...])
        score = jnp.where(forced, FORCE_SCORE, jnp.where(allowed, p_slc, -FORCE_SCORE))
        chosen = jnp.zeros((tq, N_BLK_LANES), jnp.bool_)
        for _ in range(n_pick):
            mx = jnp.max(score, axis=1, keepdims=True)
            first = jnp.min(jnp.where(score == mx, blk, 4096), axis=1, keepdims=True)
            pick = blk == first
            chosen = chosen | pick
            score = jnp.where(pick, -3e38, score)
        mneg_ref[0, :, hk * 256:(hk + 1) * 256] = jnp.where(chosen, 0.0, NEG_INF).astype(bf16)


def _cmp_attn(qn, gates, slab, bd, m5, tq, pos0):
    b, t, _ = qn.shape
    n_cmp = slab.shape[1]
    n_pick = SEL_TOPK if (pos0 + t - 1) // SEL_BLOCK < N_BLK_LANES else SEL_TOPK - 1
    return pl.pallas_call(
        functools.partial(_cmp_body, tq, pos0, n_pick),
        grid=(b, t // tq),
        in_specs=[pl.BlockSpec((1, tq, 768), lambda bb, i: (bb, i, 0)),
                  pl.BlockSpec((1, tq, 128), lambda bb, i: (bb, i, 0)),
                  pl.BlockSpec((1, n_cmp, CMP_OUT), lambda bb, i: (bb, 0, 0)),
                  pl.BlockSpec((16, 128), lambda bb, i: (0, 0)),
                  pl.BlockSpec((n_cmp, N_BLK_LANES), lambda bb, i: (0, 0))],
        out_specs=[pl.BlockSpec((1, tq, 768), lambda bb, i: (bb, i, 0)),
                   pl.BlockSpec((1, tq, 512), lambda bb, i: (bb, i, 0))],
        out_shape=[jax.ShapeDtypeStruct((b, t, 768), f32), jax.ShapeDtypeStruct((b, t, 512), bf16)],
        compiler_params=_params(("parallel", "parallel")),
        name="cmp_attn",
    )(qn, gates, slab, bd, m5)


def _stack_q(qn_ref, hk):
    return jnp.concatenate([qn_ref[:, (hk * NSA_GROUP + g) * 128:(hk * NSA_GROUP + g + 1) * 128]
                            for g in range(NSA_GROUP)], axis=0)


def _stack_rows(ref, hk):
    return jnp.concatenate([ref[hk * NSA_GROUP + g] for g in range(NSA_GROUP)], axis=0)


def _far_rows(far_ref, hk):
    return jnp.concatenate([jnp.broadcast_to(far_ref[0:1, hk * NSA_GROUP + g:hk * NSA_GROUP + g + 1], (TQ, 128))
                            for g in range(NSA_GROUP)], axis=0)


def _sel_body(qn_ref, mneg_ref, gate_ref, kaug_ref, v_ref, d0_ref, d1_ref, far_ref, os_ref,
              lhs_ref, cb_ref, m_ref, acc_ref):
    i = pl.program_id(0)
    _init_state(m_ref, acc_ref)
    for hk in range(NSA_KV_HEADS):
        q = _stack_q(qn_ref, hk)
        mn = mneg_ref[:, hk * 256:(hk + 1) * 256]
        lhs_ref[hk, 0] = jnp.concatenate([jnp.concatenate([mn[:, :128]] * NSA_GROUP, axis=0), q], axis=1)
        lhs_ref[hk, 1] = jnp.concatenate([jnp.concatenate([mn[:, 128:]] * NSA_GROUP, axis=0), q], axis=1)
        cb_ref[hk] = _far_rows(far_ref, hk)

    def tile(hk, j, half, bias_ref):
        k0 = pl.multiple_of(j * TQ, TQ)
        s = _dot_nt(lhs_ref[hk, half], kaug_ref[pl.ds(k0, TQ), :])
        vt = v_ref[pl.ds(k0, TQ), hk * 128:(hk + 1) * 128]
        if bias_ref is None:
            _flash_tile(s, cb_ref[hk], vt, m_ref, acc_ref, hk)
        else:
            _flash_tile(s + _stack_rows(bias_ref, hk), 0.0, vt, m_ref, acc_ref, hk)

    n_far = jnp.maximum(i - 1, 0)
    split = 128 * SEL_BLOCK // TQ

    def far(half):
        def body(j, c):
            for hk in range(NSA_KV_HEADS):
                tile(hk, j, half, None)
            return c
        return body

    lax.fori_loop(0, jnp.minimum(n_far, split), far(0), 0)
    lax.fori_loop(split, jnp.maximum(n_far, split), far(1), 0)

    def near(j, bias_ref):
        for half in range(2):
            @pl.when((j >= split) == (half == 1))
            def _():
                for hk in range(NSA_KV_HEADS):
                    tile(hk, j, half, bias_ref)

    @pl.when(i >= 1)
    def _():
        near(i - 1, d1_ref)

    near(i, d0_ref)
    for hk in range(NSA_KV_HEADS):
        o = _norm_low(acc_ref[hk])
        for g in range(NSA_GROUP):
            h = hk * NSA_GROUP + g
            os_ref[:, h * 128:(h + 1) * 128] = o[g * TQ:(g + 1) * TQ] * gate_ref[:, h * 3 + 1:h * 3 + 2]


def _sel_attn(qn, mneg, gates, kaug, v, d0, d1, far):
    t = qn.shape[0]
    rows = NSA_GROUP * TQ
    tile_spec = pl.BlockSpec((NSA_HEADS, TQ, TQ), lambda i: (0, 0, 0))
    return pl.pallas_call(
        _sel_body,
        grid=(t // TQ,),
        in_specs=[pl.BlockSpec((TQ, 768), lambda i: (i, 0)),
                  pl.BlockSpec((TQ, 512), lambda i: (i, 0)),
                  pl.BlockSpec((TQ, 128), lambda i: (i, 0)),
                  _resident((t, 256)), _resident((t, 256)),
                  tile_spec, tile_spec,
                  pl.BlockSpec((8, 128), lambda i: (0, 0))],
        out_specs=pl.BlockSpec((TQ, 768), lambda i: (i, 0)),
        out_shape=jax.ShapeDtypeStruct((t, 768), f32),
        scratch_shapes=[pltpu.VMEM((NSA_KV_HEADS, 2, rows, 256), bf16),
                        pltpu.VMEM((NSA_KV_HEADS, rows, 128), f32),
                        pltpu.VMEM((NSA_KV_HEADS, rows, 128), f32),
                        pltpu.VMEM((NSA_KV_HEADS, rows, 128), f32)],
        compiler_params=_params(("parallel",)),
        name="sel_attn",
    )(qn, mneg, gates, kaug, v, d0, d1, far)


def _win_body(qn_ref, gate_ref, k_ref, v_ref, d0_ref, d1_ref, d2_ref, ow_ref, m_ref, acc_ref):
    i = pl.program_id(0)
    _init_state(m_ref, acc_ref)

    def tiles(j, bias_ref):
        k0 = pl.multiple_of(j * TQ, TQ)
        kt = k_ref[pl.ds(k0, TQ), :]
        for hk in range(NSA_KV_HEADS):
            s = _dot_nt(_stack_q(qn_ref, hk), kt) + _stack_rows(bias_ref, hk)
            _flash_tile(s, 0.0, v_ref[pl.ds(k0, TQ), hk * 128:(hk + 1) * 128], m_ref, acc_ref, hk)

    @pl.when(i >= 2)
    def _():
        tiles(i - 2, d2_ref)

    @pl.when(i >= 1)
    def _():
        tiles(i - 1, d1_ref)

    tiles(i, d0_ref)
    for hk in range(NSA_KV_HEADS):
        o = _norm_low(acc_ref[hk])
        for g in range(NSA_GROUP):
            h = hk * NSA_GROUP + g
            ow_ref[:, h * 128:(h + 1) * 128] = o[g * TQ:(g + 1) * TQ] * gate_ref[:, h * 3 + 2:h * 3 + 3]


def _win_attn(qn, gates, k, v, d0, d1, d2):
    t = qn.shape[0]
    rows = NSA_GROUP * TQ
    tile_spec = pl.BlockSpec((NSA_HEADS, TQ, TQ), lambda i: (0, 0, 0))
    return pl.pallas_call(
        _win_body,
        grid=(t // TQ,),
        in_specs=[pl.BlockSpec((TQ, 768), lambda i: (i, 0)),
                  pl.BlockSpec((TQ, 128), lambda i: (i, 0)),
                  _resident((t, 128)), _resident((t, 256)),
                  tile_spec, tile_spec, tile_spec],
        out_specs=pl.BlockSpec((TQ, 768), lambda i: (i, 0)),
        out_shape=jax.ShapeDtypeStruct((t, 768), f32),
        scratch_shapes=[pltpu.VMEM((NSA_KV_HEADS, rows, 128), f32), pltpu.VMEM((NSA_KV_HEADS, rows, 128), f32)],
        compiler_params=_params(("parallel",)),
        name="win_attn",
    )(qn, gates, k, v, d0, d1, d2)


def _diff_lambda(dl, lam_init):
    return (jnp.exp(jnp.sum(dl[0:1] * dl[1:2], axis=1, keepdims=True))
            - jnp.exp(jnp.sum(dl[2:3] * dl[3:4], axis=1, keepdims=True)) + lam_init)


def _subln(od, lam_init, sg):
    y = od * lax.rsqrt(jnp.sum(od * od, axis=1, keepdims=True) * (1.0 / DIFF_V) + EPS) * sg
    return y * (1.0 - lam_init)


def _diff_body(lam_init, qd_ref, k_ref, v_ref, d0_ref, d1_ref, far_ref, dl_ref, sg_ref, od_ref, q_ref, m_ref, acc_ref):
    i = pl.program_id(0)
    lane = _lane(TQ)
    lam = _diff_lambda(dl_ref[...], lam_init)
    _init_state(m_ref, acc_ref)
    for h in range(DIFF_HEADS):
        qc = qd_ref[:, h * 128:(h + 1) * 128]
        o = (h % 2) * 64
        zero = jnp.zeros_like(qc)
        q_ref[h] = jnp.concatenate([jnp.where((lane >= o) & (lane < o + DIFF_QK), qc, zero),
                                    jnp.where((lane >= o + DIFF_QK) & (lane < o + 64), qc, zero)], axis=0)

    def tile(h, j, bias_ref):
        k0 = pl.multiple_of(j * TQ, TQ)
        s = _dot_nt(q_ref[h], k_ref[pl.ds(k0, TQ), (h // 2) * 128:(h // 2 + 1) * 128])
        vt = v_ref[pl.ds(k0, TQ), h * 128:(h + 1) * 128]
        if bias_ref is None:
            _flash_tile(s, far_ref[0:1, h:h + 1], vt, m_ref, acc_ref, h)
        else:
            b = bias_ref[h]
            _flash_tile(s + jnp.concatenate([b, b], axis=0), 0.0, vt, m_ref, acc_ref, h)

    def far(j, c):
        for h in range(DIFF_HEADS):
            tile(h, j, None)
        return c

    lax.fori_loop(0, jnp.maximum(i - 1, 0), far, 0)

    @pl.when(i >= 1)
    def _():
        for h in range(DIFF_HEADS):
            tile(h, i - 1, d1_ref)

    for h in range(DIFF_HEADS):
        tile(h, i, d0_ref)
    for h in range(DIFF_HEADS):
        o = _norm_low(acc_ref[h])
        od_ref[:, h * 128:(h + 1) * 128] = _subln(o[:TQ] - lam * o[TQ:], lam_init, sg_ref[...])


def _diff_attn(qd, k, v, d0, d1, far, dl, sg, lam_init):
    t = qd.shape[0]
    tile_spec = pl.BlockSpec((DIFF_HEADS, TQ, TQ), lambda i: (0, 0, 0))
    return pl.pallas_call(
        functools.partial(_diff_body, lam_init),
        grid=(t // TQ,),
        in_specs=[pl.BlockSpec((TQ, 512), lambda i: (i, 0)),
                  _resident((t, 256)), _resident((t, 512)),
                  tile_spec, tile_spec,
                  pl.BlockSpec((8, 128), lambda i: (0, 0)),
                  pl.BlockSpec((4, DIFF_QK), lambda i: (0, 0)),
                  pl.BlockSpec((1, 128), lambda i: (0, 0))],
        out_specs=pl.BlockSpec((TQ, 512), lambda i: (i, 0)),
        out_shape=jax.ShapeDtypeStruct((t, 512), f32),
        scratch_shapes=[pltpu.VMEM((DIFF_HEADS, 2 * TQ, 128), bf16),
                        pltpu.VMEM((DIFF_HEADS, 2 * TQ, 128), f32),
                        pltpu.VMEM((DIFF_HEADS, 2 * TQ, 128), f32)],
        compiler_params=_params(("parallel",)),
        name="diff_attn",
    )(qd, k, v, d0, d1, far, dl, sg)


def _rows48(ref):
    x = ref[0].astype(f32)
    return jnp.concatenate([x[:, h * 128:(h + 1) * 128] for h in range(NSA_HEADS)], axis=0)


def _pad_keys(new):
    return jnp.concatenate([new, jnp.zeros((PAGE - new.shape[0], new.shape[1]), f32)], axis=0)


def _sel_dec_body(*refs):
    pages = refs[1:1 + DEC_PAGES]
    (qn_ref, mneg_ref, gate_ref, new_ref, z_ref, blast_ref, bnew_ref, far_ref, os_ref,
     m_ref, l_ref, acc_ref) = refs[1 + DEC_PAGES:]
    g = pl.program_id(1)
    last = pl.num_programs(1) - 1

    @pl.when(g == 0)
    def _():
        _init_state(m_ref, acc_ref, l_ref)

    q = _rows48(qn_ref).astype(bf16)
    mn = mneg_ref[0].astype(f32)
    mrows = jnp.concatenate([mn[:, (h // NSA_GROUP) * 256:(h // NSA_GROUP + 1) * 256] for h in range(NSA_HEADS)],
                            axis=0).astype(bf16)
    kv = jnp.concatenate([p[0, 0] for p in pages], axis=0).astype(bf16)
    blocks_per_step = DEC_KEYS // SEL_BLOCK
    z = z_ref[pl.ds(pl.multiple_of(N_BLK_LANES - g * blocks_per_step, blocks_per_step), N_BLK_LANES), :]
    cb = jnp.concatenate([jnp.broadcast_to(far_ref[0:1, h:h + 1], (8, DEC_KEYS)) for h in range(NSA_HEADS)], axis=0)
    s = _dot_nt(q, kv[:, 0:128]) + _dot(mrows, z) + jnp.where(g == last, blast_ref[...], cb)
    _flash_tile(s, 0.0, kv[:, 128:256], m_ref, acc_ref, 0, l_ref)

    @pl.when(g == last)
    def _():
        kn = _pad_keys(new_ref[0]).astype(bf16)
        _flash_tile(_dot_nt(q, kn[:, 0:128]) + bnew_ref[...], 0.0, kn[:, 128:256], m_ref, acc_ref, 0, l_ref)
        o = acc_ref[0] / jnp.maximum(l_ref[0], TINY)
        low = _lane(8) < 64
        for h in range(NSA_HEADS):
            oh = o[h * 8:(h + 1) * 8]
            if h // NSA_GROUP == 1:
                oh = pltpu.roll(oh, 64, 1)
            os_ref[0, :, h * 128:(h + 1) * 128] = jnp.where(low, oh * gate_ref[0, :, h * 3 + 1:h * 3 + 2], 0.0)


def _paged_specs(width):
    def page_spec(i):
        return pl.BlockSpec((1, 1, PAGE, width), lambda bb, g, tab: (tab[bb, g * DEC_PAGES + i], 0, 0, 0))
    return [page_spec(i) for i in range(DEC_PAGES)]


def _sel_dec(pool, table, qn, mneg, gates, new, zbig, blast, bnew, far):
    b, n_pages = table.shape
    c3 = lambda wd: pl.BlockSpec((1, 8, wd), lambda bb, g, tab: (bb, 0, 0))
    full = lambda a: pl.BlockSpec(a.shape, lambda bb, g, tab: (0,) * a.ndim)
    gs = pltpu.PrefetchScalarGridSpec(
        num_scalar_prefetch=1,
        grid=(b, n_pages // DEC_PAGES),
        in_specs=_paged_specs(256) + [c3(768), c3(512), c3(128), c3(256), full(zbig), full(blast), full(bnew),
                                      full(far)],
        out_specs=c3(768),
        scratch_shapes=[pltpu.VMEM((1, 48, 128), f32), pltpu.VMEM((1, 48, 128), f32), pltpu.VMEM((1, 48, 128), f32)],
    )
    pool4 = pool.reshape(pool.shape[0], 1, PAGE, 256)
    return pl.pallas_call(
        _sel_dec_body,
        grid_spec=gs,
        out_shape=jax.ShapeDtypeStruct((b, 8, 768), f32),
        compiler_params=_params(("parallel", "arbitrary")),
        name="sel_dec",
    )(table, *([pool4] * DEC_PAGES), qn, mneg, gates, new, zbig, blast, bnew, far)


def _win_dec_body(qn_ref, gate_ref, past_ref, new_ref, bias_ref, ow_ref, keep_ref):
    q = _rows48(qn_ref).astype(bf16)
    past = past_ref[0]
    n_past = past.shape[0]
    kv = jnp.concatenate([past, _pad_keys(new_ref[0])], axis=0).astype(bf16)
    s = _dot_nt(q, kv[:, 0:128]) + bias_ref[...]
    p = jnp.exp2(s - jnp.max(s, axis=1, keepdims=True))
    o = _dot(p.astype(bf16), kv[:, 128:256]) / jnp.maximum(jnp.sum(p, axis=1, keepdims=True), TINY)
    low = _lane(8) < 64
    for h in range(NSA_HEADS):
        oh = o[h * 8:(h + 1) * 8]
        if h // NSA_GROUP == 1:
            oh = pltpu.roll(oh, 64, 1)
        ow_ref[0, :, h * 128:(h + 1) * 128] = jnp.where(low, oh * gate_ref[0, :, h * 3 + 2:h * 3 + 3], 0.0)
    keep_ref[0, 0:n_past - 8, :] = past[8:, :]
    keep_ref[0, n_past - 8:n_past, :] = new_ref[0]


def _win_dec(qn, gates, past, new, bias):
    b, n_past, _ = past.shape
    c3 = lambda wd: pl.BlockSpec((1, 8, wd), lambda bb: (bb, 0, 0))
    return pl.pallas_call(
        _win_dec_body,
        grid=(b,),
        in_specs=[c3(768), c3(128), pl.BlockSpec((1, n_past, 256), lambda bb: (bb, 0, 0)), c3(256),
                  pl.BlockSpec(bias.shape, lambda bb: (0, 0))],
        out_specs=[c3(768), pl.BlockSpec((1, n_past, 256), lambda bb: (bb, 0, 0))],
        out_shape=[jax.ShapeDtypeStruct((b, 8, 768), f32), jax.ShapeDtypeStruct((b, n_past, 256), f32)],
        compiler_params=_params(("parallel",)),
        name="win_dec",
    )(qn, gates, past, new, bias)


def _diff_dec_body(lam_init, *refs):
    pages = refs[1:1 + DEC_PAGES]
    (qd_ref, new_ref, blast_ref, bnew_ref, far_ref, dl_ref, sg_ref, od_ref, m_ref, l_ref, acc_ref) = refs[1 + DEC_PAGES:]
    g = pl.program_id(1)
    last = pl.num_programs(1) - 1

    @pl.when(g == 0)
    def _():
        _init_state(m_ref, acc_ref, l_ref)

    lane = _lane(8)
    x = qd_ref[0].astype(f32)
    rows = []
    zero = jnp.zeros((8, 128), f32)
    for h in range(DIFF_HEADS):
        qc = x[:, h * 128:(h + 1) * 128]
        if h % 2 == 1:
            qc = pltpu.roll(qc, 64, 1)
        for half in range(2):
            piece = jnp.where((lane >= half * DIFF_QK) & (lane < (half + 1) * DIFF_QK), qc, 0.0)
            rows.append(jnp.concatenate([zero] * h + [piece] + [zero] * (DIFF_HEADS - 1 - h), axis=1))
    q = jnp.concatenate(rows, axis=0).astype(bf16)
    kv = jnp.concatenate([p[0, 0] for p in pages], axis=0).astype(bf16)
    cb = jnp.concatenate([jnp.broadcast_to(far_ref[0:1, h:h + 1], (16, DEC_KEYS)) for h in range(DIFF_HEADS)], axis=0)
    s = _dot_nt(q, kv) + jnp.where(g == last, blast_ref[...], cb)
    _flash_tile(s, 0.0, kv, m_ref, acc_ref, 0, l_ref)

    @pl.when(g == last)
    def _():
        kn = _pad_keys(new_ref[0]).astype(bf16)
        _flash_tile(_dot_nt(q, kn) + bnew_ref[...], 0.0, kn, m_ref, acc_ref, 0, l_ref)
        lam = _diff_lambda(dl_ref[...], lam_init)
        l = l_ref[0]
        for h in range(DIFF_HEADS):
            a = acc_ref[0, h * 16:(h + 1) * 16, h * 128:(h + 1) * 128] / jnp.maximum(l[h * 16:(h + 1) * 16], TINY)
            od = pltpu.roll(a[0:8] - lam * a[8:16], 64, 1)
            od_ref[0, :, h * 128:(h + 1) * 128] = _subln(jnp.where(lane < 64, od, 0.0), lam_init, sg_ref[...])


def _diff_dec(pool, table, qd, new, blast, bnew, far, dl, sg, lam_init):
    b, n_pages = table.shape
    c3 = lambda wd: pl.BlockSpec((1, 8, wd), lambda bb, g, tab: (bb, 0, 0))
    full = lambda a: pl.BlockSpec(a.shape, lambda bb, g, tab: (0,) * a.ndim)
    gs = pltpu.PrefetchScalarGridSpec(
        num_scalar_prefetch=1,
        grid=(b, n_pages // DEC_PAGES),
        in_specs=_paged_specs(512) + [c3(512), c3(512), full(blast), full(bnew), full(far), full(dl), full(sg)],
        out_specs=c3(512),
        scratch_shapes=[pltpu.VMEM((1, 64, 128), f32), pltpu.VMEM((1, 64, 128), f32), pltpu.VMEM((1, 64, 512), f32)],
    )
    pool4 = pool.reshape(pool.shape[0], 1, PAGE, 512)
    return pl.pallas_call(
        functools.partial(_diff_dec_body, lam_init),
        grid_spec=gs,
        out_shape=jax.ShapeDtypeStruct((b, 8, 512), f32),
        compiler_params=_params(("parallel", "arbitrary")),
        name="diff_dec",
    )(table, *([pool4] * DEC_PAGES), qd, new, blast, bnew, far, dl, sg)


LRU_TT = 256


def _lru_body(tt, x_ref, gi_ref, cbuf_ref, h0_ref, cw_ref, cb_ref, wa_ref, ba_ref, wx_ref, bx_ref, lam_ref,
              y_ref, hl_ref, cn_ref, xe_ref, a_ref, b_ref, h_ref):
    j = pl.program_id(1)

    @pl.when(j == 0)
    def _():
        xe_ref[0:8, :] = jnp.zeros((8, LRU_WIDTH), f32)
        xe_ref[5:8, :] = cbuf_ref[0]
        h_ref[0:1, :] = h0_ref[0]

    x = x_ref[0]
    xe_ref[8:8 + tt, :] = x
    xc = cb_ref[...] + sum(xe_ref[5 + k:5 + k + tt, :] * cw_ref[k:k + 1, :] for k in range(CONV_W))
    xcb = xc.astype(bf16)
    r = _sigmoid(_dot(xcb, wa_ref[...]) + ba_ref[...])
    ig = _sigmoid(_dot(xcb, wx_ref[...]) + bx_ref[...])
    lam = lam_ref[...]
    softplus = jnp.maximum(-lam, 0.0) + jnp.log(1.0 + jnp.exp(-jnp.abs(lam)))
    log_a = -LRU_C * r * softplus
    a = jnp.exp(log_a)
    a_ref[...] = a
    b_ref[...] = jnp.sqrt(1.0 - jnp.exp(2.0 * log_a)) * (ig * xc)

    def step(k, h):
        h = a_ref[pl.ds(k, 1), :] * h + b_ref[pl.ds(k, 1), :]
        b_ref[pl.ds(k, 1), :] = h
        return h

    h = lax.fori_loop(0, tt, step, h_ref[0:1, :])
    h_ref[0:1, :] = h
    y_ref[0] = b_ref[...] * _gelu(gi_ref[0])
    hl_ref[0] = h
    cn_ref[0] = xe_ref[5 + tt:8 + tt, :]
    xe_ref[5:8, :] = xe_ref[5 + tt:8 + tt, :]


def _rglru(x, gate_in, conv_buf, h0, cw, cb, wa, ba, wx, bx, lam):
    b, t, w = x.shape
    tt = min(LRU_TT, t)

    def bd(wblk):
        m = jnp.zeros((w, w), f32)
        for n in range(LRU_BLOCKS):
            m = m.at[n * LRU_BLOCK:(n + 1) * LRU_BLOCK, n * LRU_BLOCK:(n + 1) * LRU_BLOCK].set(wblk[n])
        return m.astype(bf16)

    vec = lambda: pl.BlockSpec((1, w), lambda bb, j: (0, 0))
    return pl.pallas_call(
        functools.partial(_lru_body, tt),
        grid=(b, t // tt),
        in_specs=[pl.BlockSpec((1, tt, w), lambda bb, j: (bb, j, 0)),
                  pl.BlockSpec((1, tt, w), lambda bb, j: (bb, j, 0)),
                  pl.BlockSpec((1, 3, w), lambda bb, j: (bb, 0, 0)),
                  pl.BlockSpec((1, 1, w), lambda bb, j: (bb, 0, 0)),
                  pl.BlockSpec((CONV_W, w), lambda bb, j: (0, 0)), vec(),
                  pl.BlockSpec((w, w), lambda bb, j: (0, 0)), vec(),
                  pl.BlockSpec((w, w), lambda bb, j: (0, 0)), vec(), vec()],
        out_specs=[pl.BlockSpec((1, tt, w), lambda bb, j: (bb, j, 0)),
                   pl.BlockSpec((1, 1, w), lambda bb, j: (bb, 0, 0)),
                   pl.BlockSpec((1, 3, w), lambda bb, j: (bb, 0, 0))],
        out_shape=[jax.ShapeDtypeStruct((b, t, w), f32), jax.ShapeDtypeStruct((b, 1, w), f32),
                   jax.ShapeDtypeStruct((b, 3, w), f32)],
        scratch_shapes=[pltpu.VMEM((tt + 8, w), f32), pltpu.VMEM((tt, w), f32), pltpu.VMEM((tt, w), f32),
                        pltpu.VMEM((8, w), f32)],
        compiler_params=_params(("parallel", "arbitrary")),
        name="rglru",
    )(x, gate_in, conv_buf, h0.reshape(b, 1, w), cw, cb.reshape(1, w), bd(wa), ba.reshape(1, w), bd(wx),
      bx.reshape(1, w), lam.reshape(1, w))


def _outproj_body(x_ref, gt_ref, oc_ref, os_ref, ow_ref, od_ref, yr_ref, w_ref, o_ref):
    on = oc_ref[...] + os_ref[...] + ow_ref[...]
    cat = jnp.concatenate([on.astype(bf16), od_ref[...].astype(bf16), yr_ref[...].astype(bf16)], axis=1)
    o_ref[...] = x_ref[...] + gt_ref[...] * _dot(cat, w_ref[...])


def _outproj(x, gt, oc, os_, ow, od, yr, w, tm):
    m = x.shape[0]
    row = lambda wd: pl.BlockSpec((tm, wd), lambda i: (i, 0))
    return pl.pallas_call(
        _outproj_body,
        grid=(m // tm,),
        in_specs=[row(D_MODEL), _rowspec(gt.shape[0], tm, D_MODEL), row(768), row(768), row(768), row(512),
                  row(384), pl.BlockSpec((N_OUT_IN, D_MODEL), lambda i: (0, 0))],
        out_specs=row(D_MODEL),
        out_shape=jax.ShapeDtypeStruct((m, D_MODEL), f32),
        compiler_params=_params(("parallel",)),
        name="outproj",
    )(x, gt, oc, os_, ow, od, yr, w)


def _ffn_body(n_exp, final, x_ref, g_ref, sc_ref, sh_ref, gt_ref, rw_ref, rb_ref, w1_ref, w3_ref, w2_ref, fg_ref,
              o_ref, h_ref, acc_ref, gate_ref):
    e, f = pl.program_id(1), pl.program_id(2)

    @pl.when((e == 0) & (f == 0))
    def _():
        h = _normmod(x_ref[...], g_ref[...], sc_ref[...], sh_ref[...])
        h_ref[...] = h.astype(bf16)
        acc_ref[...] = jnp.zeros_like(acc_ref)
        if n_exp > 1:
            logits = jnp.dot(h, rw_ref[...], preferred_element_type=f32, precision=lax.Precision.HIGHEST)
            lane = lax.broadcasted_iota(jnp.int32, logits.shape, 1)
            logits = jnp.where(lane < n_exp, logits + rb_ref[...], NEG_INF)
            v1 = jnp.max(logits, axis=1, keepdims=True)
            i1 = jnp.min(jnp.where(logits == v1, lane, 4096), axis=1, keepdims=True)
            rest = jnp.where(lane == i1, NEG_INF, logits)
            v2 = jnp.max(rest, axis=1, keepdims=True)
            i2 = jnp.min(jnp.where(rest == v2, lane, 4096), axis=1, keepdims=True)
            e2 = jnp.exp(v2 - v1)
            w_1 = 1.0 / (1.0 + e2)
            w_2 = e2 / (1.0 + e2)
            gate = jnp.where(lane == i1, w_1, 0.0) + jnp.where(lane == i2, w_2, 0.0)
            for k in range(n_exp):
                gate_ref[k] = gate[:, k:k + 1]

    hb = h_ref[...]
    hid = _silu(_dot(hb, w1_ref[0])) * _dot(hb, w3_ref[0])
    if n_exp > 1:
        hid = hid * gate_ref[e]
    acc_ref[...] += _dot(hid.astype(bf16), w2_ref[0])

    @pl.when((e == n_exp - 1) & (f == pl.num_programs(2) - 1))
    def _():
        y = x_ref[...] + gt_ref[...] * acc_ref[...]
        if final:
            y = y * lax.rsqrt(jnp.mean(y * y, axis=-1, keepdims=True) + EPS) * fg_ref[...]
        o_ref[...] = y


def _ffn(x, g, sc, sh, gt, rw, rb, w1, w3, w2, fg, final, tm, tf):
    m = x.shape[0]
    n_exp, _, ff = w1.shape
    vec = lambda: pl.BlockSpec((1, D_MODEL), lambda i, e, f: (0, 0))

    def rowspec(a):
        if a.shape[0] == 1:
            return vec()
        return pl.BlockSpec((tm, D_MODEL), lambda i, e, f: (i, 0))

    return pl.pallas_call(
        functools.partial(_ffn_body, n_exp, final),
        grid=(m // tm, n_exp, ff // tf),
        in_specs=[pl.BlockSpec((tm, D_MODEL), lambda i, e, f: (i, 0)), vec(), rowspec(sc), rowspec(sh), rowspec(gt),
                  pl.BlockSpec((D_MODEL, 128), lambda i, e, f: (0, 0)),
                  pl.BlockSpec((1, 128), lambda i, e, f: (0, 0)),
                  pl.BlockSpec((1, D_MODEL, tf), lambda i, e, f: (e, 0, f)),
                  pl.BlockSpec((1, D_MODEL, tf), lambda i, e, f: (e, 0, f)),
                  pl.BlockSpec((1, tf, D_MODEL), lambda i, e, f: (e, f, 0)),
                  vec()],
        out_specs=pl.BlockSpec((tm, D_MODEL), lambda i, e, f: (i, 0)),
        out_shape=jax.ShapeDtypeStruct((m, D_MODEL), f32),
        scratch_shapes=[pltpu.VMEM((tm, D_MODEL), bf16), pltpu.VMEM((tm, D_MODEL), f32),
                        pltpu.VMEM((N_EXPERTS, tm, 1), f32)],
        compiler_params=_params(("parallel", "arbitrary", "arbitrary")),
        name="ffn",
    )(x, g.reshape(1, -1), sc, sh, gt, rw, rb, w1, w3, w2, fg.reshape(1, -1))


def _t5_bucket(dist):
    n = jnp.maximum(dist, 0)
    exact = N_BUCKETS // 2
    nf = jnp.maximum(n, 1).astype(f32)
    large = exact + (jnp.log(nf / exact) / math.log(MAX_DISTANCE / exact) * (N_BUCKETS - exact)).astype(jnp.int32)
    return jnp.where(n < exact, n, jnp.minimum(large, N_BUCKETS - 1))


def _by_dist(bd, dist, valid):
    v = jnp.where(valid[..., None], bd[jnp.clip(dist, 0, 127)], NEG_INF)
    return jnp.moveaxis(v, -1, 0)


def _bias_tables(rel_bias, pos0_dec):
    nh = rel_bias.shape[1]
    bd = rel_bias[_t5_bucket(jnp.arange(128, dtype=jnp.int32))] * LOG2E
    far = rel_bias[N_BUCKETS - 1] * LOG2E
    r = jnp.arange(TQ, dtype=jnp.int32)[:, None]
    c = jnp.arange(TQ, dtype=jnp.int32)[None, :]
    d0 = _by_dist(bd, r - c, r - c >= 0)
    d1 = _by_dist(bd, r - c + TQ, r - c + TQ >= 0)
    d2 = jnp.where((c > r)[None], far[:, None, None], NEG_INF)
    bd_pad = jnp.zeros((16, 128), f32).at[:nh].set(bd.T)
    far_pad = jnp.zeros((8, 128), f32).at[0, :nh].set(far)
    r8 = jnp.arange(8, dtype=jnp.int32)[:, None]
    ck = jnp.arange(DEC_KEYS, dtype=jnp.int32)[None, :]
    last = _by_dist(bd, DEC_KEYS + r8 - ck, jnp.ones((8, DEC_KEYS), bool))
    cn = jnp.arange(PAGE, dtype=jnp.int32)[None, :]
    new = _by_dist(bd, r8 - cn, (r8 - cn >= 0) & (cn < 8))
    n_win = min(WINDOW, pos0_dec)
    cw = jnp.arange(n_win, dtype=jnp.int32)[None, :]
    dw = n_win + r8 - cw
    win = _by_dist(bd, dw, dw < WINDOW)
    return d0, d1, d2, bd_pad, far_pad, far, last, new, win


def _m5(n_cmp):
    n = np.arange(n_cmp)[:, None]
    j = np.arange(N_BLK_LANES)[None, :]
    return jnp.asarray(((n >= 4 * j) & (n <= 4 * j + 4)).astype(np.float32), bf16)


def _zbig():
    rho = np.arange(2 * N_BLK_LANES)[:, None]
    c = np.arange(DEC_KEYS)[None, :]
    return jnp.asarray((rho == N_BLK_LANES + c // SEL_BLOCK).astype(np.float32), bf16)


def _split6(mod):
    return [mod[:, k * D_MODEL:(k + 1) * D_MODEL] for k in range(6)]


def _prompt_layer(x, mod, lw, tb, l, final_g, last):
    t = x.shape[0]
    sh1, sc1, gt1, sh2, sc2, gt2 = _split6(mod)
    (qn, qd, gates, cmp, sel, win, diff, xr, gr, kaug, vsel, kwin, vwin, kdiff, vdiff) = _proj(
        x, lw['norm_mix_g'], sc1, sh1, lw['w_aug'], lw['onerow'], 256)
    n_pages = t // PAGE
    ident = jnp.arange(n_pages, dtype=jnp.int32).reshape(1, n_pages)
    slab = _compress(cmp.reshape(n_pages, 8, 4096), ident, *lw['cmp'])
    oc, mneg = _cmp_attn(qn[None], gates[None], slab, tb['bd_nsa'], tb['m5_p'], TQ, 0)
    os_ = _sel_attn(qn, mneg[0], gates, kaug, vsel, tb['d0n'], tb['d1n'], tb['far_n'])
    ow = _win_attn(qn, gates, kwin, vwin, tb['d0n'], tb['d1n'], tb['d2n'])
    lam_init = 0.8 - 0.6 * math.exp(-0.3 * l)
    od = _diff_attn(qd, kdiff, vdiff, tb['d0d'], tb['d1d'], tb['far_d'], lw['diff_lambda'], lw['subln'], lam_init)
    zeros_c = jnp.zeros((1, CONV_W - 1, LRU_WIDTH), f32)
    zeros_h = jnp.zeros((1, LRU_WIDTH), f32)
    yr, hl, cn = _rglru(xr[None], gr[None], zeros_c, zeros_h, *lw['lru'])
    x = _outproj(x, gt1, oc[0], os_, ow, od, yr[0], lw['w_out'], 512)
    x = _ffn(x, lw['norm_ffn_g'], sc2, sh2, gt2, *lw['ffn'], final_g, last, 512, 1408)
    n_keep = min(WINDOW, t)
    state = (cmp.reshape(1, t, 2, NSA_KV_HEADS, HEAD_DIM), sel.reshape(1, t, 2, NSA_KV_HEADS, HEAD_DIM),
             diff.reshape(1, t, DIFF_HEADS, DIFF_ROW), win[t - n_keep:].reshape(1, n_keep, 2, NSA_KV_HEADS, HEAD_DIM),
             hl.reshape(1, LRU_WIDTH), cn)
    return x, state


def _sample_layer(x, mod, lw, tb, l, final_g, last, caches, page_table, nb, dec):
    pool_cmp, pool_sel, pool_diff, win_past, h0, conv0 = caches
    pos0 = page_table.shape[1] * PAGE
    rep = lambda a: jnp.repeat(a, dec, axis=0)
    sh1, sc1, gt1, sh2, sc2, gt2 = [rep(m) for m in _split6(mod)]
    m = nb * dec
    (qn, qd, gates, cmp, sel, win, diff, xr, gr, _, _, _, _, _, _) = _proj(
        x, lw['norm_mix_g'], sc1, sh1, lw['w_aug'], lw['onerow'], m)
    b3 = lambda a: a.reshape(nb, dec, a.shape[-1])
    slab = _compress(pool_cmp.reshape(-1, 8, 4096), page_table, *lw['cmp'])
    oc, mneg = _cmp_attn(b3(qn), b3(gates), slab, tb['bd_nsa'], tb['m5_s'], dec, pos0)
    os_ = _sel_dec(pool_sel, page_table, b3(qn), mneg, b3(gates), b3(sel), tb['zbig'], tb['last_n'], tb['new_n'],
                   tb['far_n'])
    ow, win_keep = _win_dec(b3(qn), b3(gates), win_past, b3(win), tb['win_n'])
    lam_init = 0.8 - 0.6 * math.exp(-0.3 * l)
    od = _diff_dec(pool_diff, page_table, b3(qd), b3(diff), tb['last_d'], tb['new_d'], tb['far_d'],
                   lw['diff_lambda'], lw['subln'], lam_init)
    yr, hl, cn = _rglru(b3(xr), b3(gr), conv0, h0, *lw['lru'])
    flat = lambda a: a.reshape(m, a.shape[-1])
    x = _outproj(x, gt1, flat(oc), flat(os_), flat(ow), flat(od), flat(yr), lw['w_out'], m)
    x = _ffn(x, lw['norm_ffn_g'], sc2, sh2, gt2, *lw['ffn'], final_g, last, m, 1408)
    n_keep = win_keep.shape[1]
    state = (cmp.reshape(nb, dec, 2, NSA_KV_HEADS, HEAD_DIM), sel.reshape(nb, dec, 2, NSA_KV_HEADS, HEAD_DIM),
             diff.reshape(nb, dec, DIFF_HEADS, DIFF_ROW), win_keep.reshape(nb, n_keep, 2, NSA_KV_HEADS, HEAD_DIM),
             hl.reshape(nb, LRU_WIDTH), cn)
    return x, state


def kernel(x_prompt, x_sample, c_prompt, c_sample, cache_nsa_cmp, cache_nsa_sel, cache_diff, cache_nsa_win, state_lru_h, state_lru_conv, page_table, rel_bias, norm_mix_g, norm_ffn_g, final_norm_g, w_ada, b_ada, w_in, cmp_pe, cmp_w1, cmp_b1, cmp_w2, diff_lambda, diff_subln_g, lru_conv_w, lru_conv_b, lru_wa, lru_ba, lru_wx, lru_bx, lru_lambda, w_out, ffn_w1, ffn_w3, ffn_w2, router_w, router_b, moe_w1, moe_w3, moe_w2):
    depth = w_in.shape[0]
    t = x_prompt.shape[1]
    nb, dec = x_sample.shape[0], x_sample.shape[1]
    n_pool = cache_nsa_cmp.shape[1]
    pos0_dec = page_table.shape[1] * PAGE
    cols, onerow = _proj_cols()
    rows = _outproj_rows()
    onerow = jnp.asarray(onerow)
    d0, d1, d2, bd, far_pad, far, last, new, win = _bias_tables(rel_bias, pos0_dec)
    nh = NSA_HEADS
    stack_rows = lambda a: a.reshape(-1, a.shape[-1])
    twice = lambda a: jnp.repeat(a, 2, axis=0)
    tb = {'d0n': d0[:nh], 'd1n': d1[:nh], 'd2n': d2[:nh], 'd0d': d0[nh:], 'd1d': d1[nh:], 'bd_nsa': bd,
          'far_n': far_pad, 'far_d': jnp.zeros((8, 128), f32).at[0, :DIFF_HEADS].set(far[nh:]),
          'm5_p': _m5(t // CMP_STRIDE), 'm5_s': _m5(pos0_dec // CMP_STRIDE), 'zbig': _zbig(),
          'last_n': stack_rows(last[:nh]), 'new_n': stack_rows(new[:nh]),
          'win_n': jnp.concatenate([stack_rows(win[:nh]), stack_rows(new[:nh])], axis=1),
          'last_d': stack_rows(twice(last[nh:])), 'new_d': stack_rows(twice(new[nh:]))}

    c_all = jnp.concatenate([c_prompt, c_sample, jnp.zeros((7, D_MODEL), f32)], axis=0)
    layers = []
    for l in range(depth):
        j = l // 2
        lw = {'norm_mix_g': norm_mix_g[l], 'norm_ffn_g': norm_ffn_g[l],
              'w_aug': _take_cols(w_in[l], cols).astype(bf16), 'onerow': onerow,
              'cmp': _compress_weights(cmp_pe[l], cmp_w1[l], cmp_b1[l], cmp_w2[l]),
              'diff_lambda': diff_lambda[l],
              'subln': jnp.zeros((1, 128), f32).at[0, :64].set(diff_subln_g[l]),
              'lru': (lru_conv_w[l], lru_conv_b[l], lru_wa[l], lru_ba[l], lru_wx[l], lru_bx[l], lru_lambda[l]),
              'w_out': _take_rows(w_out[l], rows).astype(bf16)}
        if l % 2 == 0:
            lw['ffn'] = (jnp.zeros((D_MODEL, 128), f32), jnp.zeros((1, 128), f32),
                         ffn_w1[j][None].astype(bf16), ffn_w3[j][None].astype(bf16), ffn_w2[j][None].astype(bf16))
        else:
            rw = jnp.zeros((D_MODEL, 128), f32).at[:, :N_EXPERTS].set(router_w[j])
            rb = jnp.zeros((1, 128), f32).at[0, :N_EXPERTS].set(router_b[j])
            lw['ffn'] = (rw, rb, moe_w1[j].astype(bf16), moe_w3[j].astype(bf16), moe_w2[j].astype(bf16))
        layers.append(lw)

    mods = [_ada(c_all, w_ada[l], b_ada[l]) for l in range(depth)]

    x = x_prompt[0]
    p_states = []
    for l in range(depth):
        x, st = _prompt_layer(x, mods[l][0:1], layers[l], tb, l, final_norm_g, l == depth - 1)
        p_states.append(st)
    y_prompt = x[None]

    xs = x_sample.reshape(nb * dec, D_MODEL)
    s_states = []
    for l in range(depth):
        caches = (cache_nsa_cmp[l].reshape(n_pool, PAGE, 256), cache_nsa_sel[l].reshape(n_pool, PAGE, 256),
                  cache_diff[l].reshape(n_pool, PAGE, 512), cache_nsa_win[l].reshape(nb, -1, 256),
                  state_lru_h[l], state_lru_conv[l])
        xs, st = _sample_layer(xs, mods[l][1:1 + nb], layers[l], tb, l, final_norm_g, l == depth - 1, caches,
                               page_table, nb, dec)
        s_states.append(st)
    y_sample = xs.reshape(nb, dec, D_MODEL)

    stack = lambda sts, k: jnp.stack([s[k] for s in sts], axis=0)
    return (y_prompt, y_sample, *[stack(p_states, k) for k in range(6)], *[stack(s_states, k) for k in range(6)])
```

```python
import functools
import math

import numpy as np
import jax
import jax.numpy as jnp
from jax import lax
from jax.experimental import pallas as pl
from jax.experimental.pallas import tpu as pltpu

f32 = jnp.float32
bf16 = jnp.bfloat16

D_MODEL = 1024
HEAD_DIM = 64
NSA_HEADS = 6
NSA_KV_HEADS = 2
NSA_GROUP = 3
CMP_STRIDE = 16
CMP_LEN = 32
CMP_HIDDEN = 128
SEL_BLOCK = 64
SEL_RATIO = 4
SEL_TOPK = 16
WINDOW = 512
DIFF_HEADS = 4
DIFF_QK = 32
DIFF_V = 64
DIFF_ROW = 128
LRU_WIDTH = 384
LRU_BLOCKS = 6
LRU_BLOCK = 64
CONV_W = 4
LRU_C = 8.0
D_NSA = 384
D_DIFF = 256
N_BUCKETS = 32
MAX_DISTANCE = 128
D_FF = 2816
N_EXPERTS = 8
D_FF_EXPERT = 1408
EPS = 1e-6
NEG_INF = -1e30
TINY = 1e-30
FORCE_SCORE = 1e4
ATT_SCALE = HEAD_DIM ** -0.5
DIFF_SCALE = DIFF_QK ** -0.5
PAGE = 128
LOG2E = 1.4426950408889634

TQ = 256
N_BLK_LANES = 256
M_INIT = -1e20
VMEM_LIMIT = 56 * 1024 * 1024
DEC_PAGES = 16
DEC_KEYS = DEC_PAGES * PAGE

(O_QN, O_QD, O_GATE, O_CMP, O_SEL, O_WIN, O_DIFF, O_XR, O_GR, O_KAUG, O_VSEL, O_KWIN, O_VWIN, O_KDIFF, O_VDIFF,
 N_PROJ) = (0, 768, 1280, 1408, 1664, 1920, 2176, 2688, 3072, 3456, 3712, 3968, 4096, 4352, 4608, 5120)
S_QN, S_GN, S_CMP, S_SEL, S_WIN, S_QD, S_KD, S_VD, S_XR, S_GR, D_IN = (
    0, 384, 402, 658, 914, 1170, 1426, 1682, 1938, 2322, 2706)
N_OUT_IN = 768 + 512 + 384


def _proj_cols():
    c, one = [], []

    def seg(cols, ones=None):
        c.extend(cols)
        one.extend(ones if ones is not None else [0.0] * len(cols))

    for h in range(NSA_HEADS):
        hk = h // NSA_GROUP
        seg([-1] * (hk * 64) + [S_QN + h * 64 + d for d in range(64)] + [-1] * (64 - hk * 64))
    for h in range(DIFF_HEADS):
        o = (h % 2) * 64
        seg([-1] * o + [S_QD + h * 64 + d for d in range(64)] + [-1] * (64 - o))
    seg([S_GN + i for i in range(18)] + [-1] * 110)
    seg(list(range(S_CMP, S_CMP + 256)) + list(range(S_SEL, S_SEL + 256)) + list(range(S_WIN, S_WIN + 256)))
    for h in range(DIFF_HEADS):
        seg([S_KD + h * 64 + d for d in range(64)] + [S_VD + h * 64 + d for d in range(64)])
    seg(list(range(S_XR, S_XR + 384)) + list(range(S_GR, S_GR + 384)))
    seg([-1] * 128 + list(range(S_SEL, S_SEL + 128)))
    for hk in range(NSA_KV_HEADS):
        seg([S_SEL + 128 + hk * 64 + d for d in range(64)] + [-1] * 64, [0.0] * 64 + [1.0] * 64)
    seg(list(range(S_WIN, S_WIN + 128)))
    for hk in range(NSA_KV_HEADS):
        seg([S_WIN + 128 + hk * 64 + d for d in range(64)] + [-1] * 64, [0.0] * 64 + [1.0] * 64)
    seg(list(range(S_KD, S_KD + 256)))
    for h in range(DIFF_HEADS):
        seg([S_VD + h * 64 + d for d in range(64)] + [-1] * 64, [0.0] * 64 + [1.0] * 64)
    assert len(c) == N_PROJ
    return np.asarray(c, np.int32), np.asarray(one, np.float32).reshape(1, N_PROJ)


def _outproj_rows():
    r = []
    for h in range(NSA_HEADS):
        r += [h * 64 + d for d in range(64)] + [-1] * 64
    for h in range(DIFF_HEADS):
        r += [D_NSA + h * 64 + d for d in range(64)] + [-1] * 64
    r += list(range(D_NSA + D_DIFF, 1024))
    assert len(r) == N_OUT_IN
    return np.asarray(r, np.int32)


def _take_cols(w, cols):
    return jnp.where(cols[None, :] >= 0, jnp.take(w, np.maximum(cols, 0), axis=1), 0.0)


def _take_rows(w, rows):
    return jnp.where(rows[:, None] >= 0, jnp.take(w, np.maximum(rows, 0), axis=0), 0.0)


def _params(sem):
    return pltpu.CompilerParams(dimension_semantics=sem, vmem_limit_bytes=VMEM_LIMIT)


def _dot(a, b):
    return jnp.dot(a, b, preferred_element_type=f32)


def _dot_nt(a, b):
    return lax.dot_general(a, b, (((1,), (1,)), ((), ())), preferred_element_type=f32)


def _gelu(x):
    return 0.5 * x * (1.0 + jnp.tanh(math.sqrt(2.0 / math.pi) * (x + 0.044715 * (x * x * x))))


def _sigmoid(x):
    return 1.0 / (1.0 + jnp.exp(-x))


def _silu(x):
    return x * _sigmoid(x)


def _normmod(x, g, sc, sh):
    h = x * lax.rsqrt(jnp.mean(x * x, axis=-1, keepdims=True) + EPS) * g
    return h * (1.0 + sc) + sh


def _rowspec(rows, tm, width):
    if rows == 1:
        return pl.BlockSpec((1, width), lambda i: (0, 0))
    return pl.BlockSpec((tm, width), lambda i: (i, 0))


def _resident(shape):
    return pl.BlockSpec(shape, lambda i: (0,) * len(shape), pipeline_mode=pl.Buffered(1))


def _lane(rows):
    return lax.broadcasted_iota(jnp.int32, (rows, 128), 1)


def _ada_body(c_ref, w_ref, b_ref, o_ref):
    c = c_ref[...]
    o_ref[...] = _dot(_silu(c).astype(bf16), w_ref[...].astype(bf16)) + b_ref[...]


def _ada(c, w, b):
    m, n = c.shape[0], w.shape[1]
    tn = 1536
    return pl.pallas_call(
        _ada_body,
        grid=(n // tn,),
        in_specs=[pl.BlockSpec((m, D_MODEL), lambda j: (0, 0)),
                  pl.BlockSpec((D_MODEL, tn), lambda j: (0, j)),
                  pl.BlockSpec((1, tn), lambda j: (0, j))],
        out_specs=pl.BlockSpec((m, tn), lambda j: (0, j)),
        out_shape=jax.ShapeDtypeStruct((m, n), f32),
        compiler_params=_params(("parallel",)),
        name="ada",
    )(c, w, b.reshape(1, n))


PROJ_OUTS = [(768, bf16), (512, bf16), (128, f32), (256, f32), (256, f32), (256, f32), (512, f32), (384, f32),
             (384, f32), (256, bf16), (256, bf16), (128, bf16), (256, bf16), (256, bf16), (512, bf16)]


def _proj_body(tm, x_ref, g_ref, sc_ref, sh_ref, w_ref, one_ref, qn_ref, qd_ref, gate_ref, cmp_ref, sel_ref,
               win_ref, diff_ref, xr_ref, gr_ref, kaug_ref, vsel_ref, kwin_ref, vwin_ref, kdiff_ref, vdiff_ref):
    h = _normmod(x_ref[...], g_ref[...], sc_ref[...], sh_ref[...])
    pr = _dot(h.astype(bf16), w_ref[...]) + one_ref[...]
    qn_ref[...] = (pr[:, O_QN:O_QD] * (ATT_SCALE * LOG2E)).astype(bf16)
    qd_ref[...] = (pr[:, O_QD:O_GATE] * (DIFF_SCALE * LOG2E)).astype(bf16)
    gate_ref[...] = _sigmoid(pr[:, O_GATE:O_CMP])
    cmp_ref[...] = pr[:, O_CMP:O_SEL]
    sel_ref[...] = pr[:, O_SEL:O_WIN]
    win_ref[...] = pr[:, O_WIN:O_DIFF]
    diff_ref[...] = pr[:, O_DIFF:O_XR]
    xr_ref[...] = pr[:, O_XR:O_GR]
    gr_ref[...] = pr[:, O_GR:O_KAUG]
    t = pl.program_id(0) * tm + lax.broadcasted_iota(jnp.int32, (tm, 128), 0)
    onehot = (_lane(tm) == (t // SEL_BLOCK) % 128).astype(f32)
    kaug_ref[:, 0:128] = onehot.astype(bf16)
    kaug_ref[:, 128:256] = pr[:, O_KAUG + 128:O_VSEL].astype(bf16)
    vsel_ref[...] = pr[:, O_VSEL:O_KWIN].astype(bf16)
    kwin_ref[...] = pr[:, O_KWIN:O_VWIN].astype(bf16)
    vwin_ref[...] = pr[:, O_VWIN:O_KDIFF].astype(bf16)
    kdiff_ref[...] = pr[:, O_KDIFF:O_VDIFF].astype(bf16)
    vdiff_ref[...] = pr[:, O_VDIFF:N_PROJ].astype(bf16)


def _proj(x, g, sc, sh, w, onerow, tm):
    m = x.shape[0]
    return pl.pallas_call(
        functools.partial(_proj_body, tm),
        grid=(m // tm,),
        in_specs=[pl.BlockSpec((tm, D_MODEL), lambda i: (i, 0)),
                  pl.BlockSpec((1, D_MODEL), lambda i: (0, 0)),
                  _rowspec(sc.shape[0], tm, D_MODEL), _rowspec(sh.shape[0], tm, D_MODEL),
                  pl.BlockSpec((D_MODEL, N_PROJ), lambda i: (0, 0)),
                  pl.BlockSpec((1, N_PROJ), lambda i: (0, 0))],
        out_specs=[pl.BlockSpec((tm, wd), lambda i: (i, 0)) for wd, _ in PROJ_OUTS],
        out_shape=[jax.ShapeDtypeStruct((m, wd), dt) for wd, dt in PROJ_OUTS],
        compiler_params=_params(("parallel",)),
        name="proj",
    )(x, g.reshape(1, -1), sc, sh, w, onerow)


CMP_PAGES = 16
CMP_ROWS = CMP_PAGES * 8
CMP_OUT = 384


def _compress_body(*refs):
    pages = refs[1:1 + CMP_PAGES]
    wc_ref, pe_ref, b1_ref, w2_ref, o_ref, carry_ref = refs[1 + CMP_PAGES:]
    g = pl.program_id(1)

    @pl.when(g == 0)
    def _():
        carry_ref[...] = jnp.zeros_like(carry_ref)

    x = jnp.concatenate([p[0, 0] for p in pages], axis=0).astype(bf16)
    a = _dot(x, wc_ref[...])
    pe = _dot(pe_ref[...].astype(bf16), wc_ref[...])
    const = pe[0:1, :512] + pe[1:2, 512:] + b1_ref[...]
    first, second = a[:, :512], a[:, 512:]
    rid = lax.broadcasted_iota(jnp.int32, (CMP_ROWS, 512), 0)
    prev_first = jnp.where(rid == 0, carry_ref[0:1, :], pltpu.roll(first, 1, 0))
    carry_ref[0:1, :] = first[CMP_ROWS - 1:CMP_ROWS, :]
    hid = _gelu(prev_first + second + const)
    o_ref[0] = _dot(hid.astype(bf16), w2_ref[...]).astype(bf16)


def _compress(pool, table, wc, pe2, b1, w2bd):
    b, n_pages = table.shape
    steps = n_pages // CMP_PAGES

    def page_spec(i):
        return pl.BlockSpec((1, 1, 8, 4096), lambda bb, g, tab: (tab[bb, g * CMP_PAGES + i], 0, 0, 0))

    gs = pltpu.PrefetchScalarGridSpec(
        num_scalar_prefetch=1,
        grid=(b, steps),
        in_specs=[page_spec(i) for i in range(CMP_PAGES)] + [
            pl.BlockSpec((4096, 1024), lambda bb, g, tab: (0, 0)),
            pl.BlockSpec((8, 4096), lambda bb, g, tab: (0, 0)),
            pl.BlockSpec((1, 512), lambda bb, g, tab: (0, 0)),
            pl.BlockSpec((512, CMP_OUT), lambda bb, g, tab: (0, 0))],
        out_specs=pl.BlockSpec((1, CMP_ROWS, CMP_OUT), lambda bb, g, tab: (bb, g, 0)),
        scratch_shapes=[pltpu.VMEM((8, 512), f32)],
    )
    pool4 = pool.reshape(pool.shape[0], 1, 8, 4096)
    return pl.pallas_call(
        _compress_body,
        grid_spec=gs,
        out_shape=jax.ShapeDtypeStruct((b, n_pages * 8, CMP_OUT), bf16),
        compiler_params=_params(("parallel", "arbitrary")),
        name="compress",
    )(table, *([pool4] * CMP_PAGES), wc, pe2, b1, w2bd)


def _compress_weights(cmp_pe, cmp_w1, cmp_b1, cmp_w2):
    w1 = cmp_w1.reshape(2, 2, CMP_STRIDE, HEAD_DIM, CMP_HIDDEN)
    eye4 = jnp.eye(4, dtype=f32)
    w1s = jnp.stack([w1[0], w1[0], w1[1], w1[1]], axis=0)
    wc = jnp.einsum('sflde,st->lsdfte', w1s, eye4).reshape(4096, 1024)
    pe = cmp_pe.reshape(2, 2, CMP_STRIDE, HEAD_DIM)
    pes = jnp.stack([pe[0], pe[0], pe[1], pe[1]], axis=0)
    pe2 = jnp.transpose(pes, (1, 2, 0, 3)).reshape(2, 4096)
    pe2 = jnp.concatenate([pe2, jnp.zeros((6, 4096), f32)], axis=0)
    b1 = jnp.stack([cmp_b1[0], cmp_b1[0], cmp_b1[1], cmp_b1[1]], axis=0).reshape(1, 512)
    w2bd = jnp.zeros((512, CMP_OUT), f32)
    for s, lane0 in ((0, 0), (1, 64), (2, 128), (3, 256)):
        w2bd = w2bd.at[s * 128:(s + 1) * 128, lane0:lane0 + 64].set(cmp_w2[s // 2])
    return wc.astype(bf16), pe2, b1, w2bd.astype(bf16)


def _lane_tile(x, reps):
    return x if reps == 1 else jnp.concatenate([x] * reps, axis=1)


def _flash_tile(s, cbm, vt, m_ref, acc_ref, idx, l_ref=None, v_transposed=False):
    m_prev = m_ref[idx]
    m_new = jnp.maximum(m_prev, jnp.max(s, axis=1, keepdims=True) + cbm)
    alpha = jnp.exp2(m_prev - m_new)
    p = jnp.exp2(s - _lane_tile(m_new - cbm, s.shape[1] // 128))
    if l_ref is not None:
        l_ref[idx] = alpha * l_ref[idx] + jnp.sum(p, axis=1, keepdims=True)
    reps = acc_ref.shape[-1] // 128
    a = _lane_tile(alpha, reps)
    pv = _dot_nt(p.astype(bf16), vt) if v_transposed else _dot(p.astype(bf16), vt)
    acc_ref[idx] = a * acc_ref[idx] + pv
    m_ref[idx] = m_new


def _init_state(m_ref, acc_ref, l_ref=None):
    m_ref[...] = jnp.full(m_ref.shape, M_INIT, f32)
    acc_ref[...] = jnp.zeros(acc_ref.shape, f32)
    if l_ref is not None:
        l_ref[...] = jnp.zeros(l_ref.shape, f32)


def _norm_low(acc):
    l = pltpu.roll(acc, 64, 1)
    return jnp.where(_lane(acc.shape[0]) < 64, acc / jnp.maximum(l, TINY), 0.0)


def _cmp_body(tq, pos0, n_pick, qn_ref, gate_ref, slab_ref, bd_ref, m5_ref, oc_ref, mneg_ref):
    i = pl.program_id(1)
    n_cmp = slab_ref.shape[1]
    t = pos0 + i * tq + lax.broadcasted_iota(jnp.int32, (tq, n_cmp), 0)
    n = lax.broadcasted_iota(jnp.int32, (tq, n_cmp), 1)
    dist = t - (CMP_STRIDE * n + CMP_STRIDE - 1)
    valid = (dist >= 0) & (n >= 1)
    idx = jnp.clip(dist, 0, 127)
    low = _lane(tq) < 64
    blk = lax.broadcasted_iota(jnp.int32, (tq, N_BLK_LANES), 1)
    tb = pos0 + i * tq + lax.broadcasted_iota(jnp.int32, (tq, N_BLK_LANES), 0)
    forced = (blk == tb // SEL_BLOCK) | (blk == 0)
    allowed = blk * SEL_BLOCK <= tb
    small = tq % 16 != 0
    kc = slab_ref[0, :, 0:128]
    kc = kc.astype(f32) if small else kc
    for hk in range(NSA_KV_HEADS):
        vc = slab_ref[0, :, 128 + hk * 128:256 + hk * 128]
        imp = jnp.zeros((tq, n_cmp), f32)
        for g in range(NSA_GROUP):
            h = hk * NSA_GROUP + g
            q = qn_ref[0, :, h * 128:(h + 1) * 128]
            s = _dot_nt(q.astype(f32) if small else q, kc)
            tab = jnp.broadcast_to(bd_ref[h:h + 1, :], (tq, 128))
            bias = jnp.concatenate(
                [jnp.take_along_axis(tab, idx[:, c * 128:(c + 1) * 128], axis=1) for c in range(n_cmp // 128)],
                axis=1)
            s = jnp.where(valid, s + bias, NEG_INF)
            p = jnp.exp2(s - jnp.max(s, axis=1, keepdims=True)) * valid.astype(f32)
            p = p / jnp.maximum(jnp.sum(p, axis=1, keepdims=True), TINY)
            imp = imp + p
            o = _dot(p, vc.astype(f32)) if small else _dot(p.astype(bf16), vc)
            gate = gate_ref[0, :, h * 3:h * 3 + 1]
            oc_ref[0, :, h * 128:(h + 1) * 128] = jnp.where(low, o * gate, 0.0)
        imp_hi = imp.astype(bf16)
        imp_lo = (imp - imp_hi.astype(f32)).astype(bf16)
        if small:
            m5 = m5_ref[...].astype(f32)
            p_slc = _dot(imp_hi.astype(f32), m5) + _dot(imp_lo.astype(f32), m5)
        else:
            p_slc = _dot(imp_hi, m5_ref[...]) + _dot(imp_lo, m5_ref[...])
        score = jnp.where(forced, FORCE_SCORE, jnp.where(allowed, p_slc, -FORCE_SCORE))
        chosen = jnp.zeros((tq, N_BLK_LANES), jnp.bool_)
        for _ in range(n_pick):
            mx = jnp.max(score, axis=1, keepdims=True)
            first = jnp.min(jnp.where(score == mx, blk, 4096), axis=1, keepdims=True)
            pick = blk == first
            chosen = chosen | pick
            score = jnp.where(pick, -3e38, score)
        mneg_ref[0, :, hk * 256:(hk + 1) * 256] = jnp.where(chosen, 0.0, NEG_INF).astype(bf16)


def _cmp_attn(qn, gates, slab, bd, m5, tq, pos0):
    b, t, _ = qn.shape
    n_cmp = slab.shape[1]
    n_pick = SEL_TOPK if (pos0 + t - 1) // SEL_BLOCK < N_BLK_LANES else SEL_TOPK - 1
    return pl.pallas_call(
        functools.partial(_cmp_body, tq, pos0, n_pick),
        grid=(b, t // tq),
        in_specs=[pl.BlockSpec((1, tq, 768), lambda bb, i: (bb, i, 0)),
                  pl.BlockSpec((1, tq, 128), lambda bb, i: (bb, i, 0)),
                  pl.BlockSpec((1, n_cmp, CMP_OUT), lambda bb, i: (bb, 0, 0)),
                  pl.BlockSpec((16, 128), lambda bb, i: (0, 0)),
                  pl.BlockSpec((n_cmp, N_BLK_LANES), lambda bb, i: (0, 0))],
        out_specs=[pl.BlockSpec((1, tq, 768), lambda bb, i: (bb, i, 0)),
                   pl.BlockSpec((1, tq, 512), lambda bb, i: (bb, i, 0))],
        out_shape=[jax.ShapeDtypeStruct((b, t, 768), f32), jax.ShapeDtypeStruct((b, t, 512), bf16)],
        compiler_params=_params(("parallel", "parallel")),
        name="cmp_attn",
    )(qn, gates, slab, bd, m5)


def _stack_q(qn_ref, hk):
    return jnp.concatenate([qn_ref[:, (hk * NSA_GROUP + g) * 128:(hk * NSA_GROUP + g + 1) * 128]
                            for g in range(NSA_GROUP)], axis=0)


def _stack_rows(ref, hk):
    return jnp.concatenate([ref[hk * NSA_GROUP + g] for g in range(NSA_GROUP)], axis=0)


def _far_rows(far_ref, hk):
    return jnp.concatenate([jnp.broadcast_to(far_ref[0:1, hk * NSA_GROUP + g:hk * NSA_GROUP + g + 1], (TQ, 128))
                            for g in range(NSA_GROUP)], axis=0)


def _sel_body(qn_ref, mneg_ref, gate_ref, kaug_ref, v_ref, d0_ref, d1_ref, far_ref, os_ref,
              lhs_ref, cb_ref, m_ref, acc_ref):
    i = pl.program_id(0)
    _init_state(m_ref, acc_ref)
    for hk in range(NSA_KV_HEADS):
        q = _stack_q(qn_ref, hk)
        mn = mneg_ref[:, hk * 256:(hk + 1) * 256]
        lhs_ref[hk, 0] = jnp.concatenate([jnp.concatenate([mn[:, :128]] * NSA_GROUP, axis=0), q], axis=1)
        lhs_ref[hk, 1] = jnp.concatenate([jnp.concatenate([mn[:, 128:]] * NSA_GROUP, axis=0), q], axis=1)
        cb_ref[hk] = _far_rows(far_ref, hk)

    def tile(hk, j, half, bias_ref):
        k0 = pl.multiple_of(j * TQ, TQ)
        s = _dot_nt(lhs_ref[hk, half], kaug_ref[pl.ds(k0, TQ), :])
        vt = v_ref[pl.ds(k0, TQ), hk * 128:(hk + 1) * 128]
        if bias_ref is None:
            _flash_tile(s, cb_ref[hk], vt, m_ref, acc_ref, hk)
        else:
            _flash_tile(s + _stack_rows(bias_ref, hk), 0.0, vt, m_ref, acc_ref, hk)

    n_far = jnp.maximum(i - 1, 0)
    split = 128 * SEL_BLOCK // TQ

    def far(half):
        def body(j, c):
            for hk in range(NSA_KV_HEADS):
                tile(hk, j, half, None)
            return c
        return body

    lax.fori_loop(0, jnp.minimum(n_far, split), far(0), 0)
    lax.fori_loop(split, jnp.maximum(n_far, split), far(1), 0)

    def near(j, bias_ref):
        for half in range(2):
            @pl.when((j >= split) == (half == 1))
            def _():
                for hk in range(NSA_KV_HEADS):
                    tile(hk, j, half, bias_ref)

    @pl.when(i >= 1)
    def _():
        near(i - 1, d1_ref)

    near(i, d0_ref)
    for hk in range(NSA_KV_HEADS):
        o = _norm_low(acc_ref[hk])
        for g in range(NSA_GROUP):
            h = hk * NSA_GROUP + g
            os_ref[:, h * 128:(h + 1) * 128] = o[g * TQ:(g + 1) * TQ] * gate_ref[:, h * 3 + 1:h * 3 + 2]


def _sel_attn(qn, mneg, gates, kaug, v, d0, d1, far):
    t = qn.shape[0]
    rows = NSA_GROUP * TQ
    tile_spec = pl.BlockSpec((NSA_HEADS, TQ, TQ), lambda i: (0, 0, 0))
    return pl.pallas_call(
        _sel_body,
        grid=(t // TQ,),
        in_specs=[pl.BlockSpec((TQ, 768), lambda i: (i, 0)),
                  pl.BlockSpec((TQ, 512), lambda i: (i, 0)),
                  pl.BlockSpec((TQ, 128), lambda i: (i, 0)),
                  _resident((t, 256)), _resident((t, 256)),
                  tile_spec, tile_spec,
                  pl.BlockSpec((8, 128), lambda i: (0, 0))],
        out_specs=pl.BlockSpec((TQ, 768), lambda i: (i, 0)),
        out_shape=jax.ShapeDtypeStruct((t, 768), f32),
        scratch_shapes=[pltpu.VMEM((NSA_KV_HEADS, 2, rows, 256), bf16),
                        pltpu.VMEM((NSA_KV_HEADS, rows, 128), f32),
                        pltpu.VMEM((NSA_KV_HEADS, rows, 128), f32),
                        pltpu.VMEM((NSA_KV_HEADS, rows, 128), f32)],
        compiler_params=_params(("parallel",)),
        name="sel_attn",
    )(qn, mneg, gates, kaug, v, d0, d1, far)


def _win_body(qn_ref, gate_ref, k_ref, v_ref, d0_ref, d1_ref, d2_ref, ow_ref, m_ref, acc_ref):
    i = pl.program_id(0)
    _init_state(m_ref, acc_ref)

    def tiles(j, bias_ref):
        k0 = pl.multiple_of(j * TQ, TQ)
        kt = k_ref[pl.ds(k0, TQ), :]
        for hk in range(NSA_KV_HEADS):
            s = _dot_nt(_stack_q(qn_ref, hk), kt) + _stack_rows(bias_ref, hk)
            _flash_tile(s, 0.0, v_ref[pl.ds(k0, TQ), hk * 128:(hk + 1) * 128], m_ref, acc_ref, hk)

    @pl.when(i >= 2)
    def _():
        tiles(i - 2, d2_ref)

    @pl.when(i >= 1)
    def _():
        tiles(i - 1, d1_ref)

    tiles(i, d0_ref)
    for hk in range(NSA_KV_HEADS):
        o = _norm_low(acc_ref[hk])
        for g in range(NSA_GROUP):
            h = hk * NSA_GROUP + g
            ow_ref[:, h * 128:(h + 1) * 128] = o[g * TQ:(g + 1) * TQ] * gate_ref[:, h * 3 + 2:h * 3 + 3]


def _win_attn(qn, gates, k, v, d0, d1, d2):
    t = qn.shape[0]
    rows = NSA_GROUP * TQ
    tile_spec = pl.BlockSpec((NSA_HEADS, TQ, TQ), lambda i: (0, 0, 0))
    return pl.pallas_call(
        _win_body,
        grid=(t // TQ,),
        in_specs=[pl.BlockSpec((TQ, 768), lambda i: (i, 0)),
                  pl.BlockSpec((TQ, 128), lambda i: (i, 0)),
                  _resident((t, 128)), _resident((t, 256)),
                  tile_spec, tile_spec, tile_spec],
        out_specs=pl.BlockSpec((TQ, 768), lambda i: (i, 0)),
        out_shape=jax.ShapeDtypeStruct((t, 768), f32),
        scratch_shapes=[pltpu.VMEM((NSA_KV_HEADS, rows, 128), f32), pltpu.VMEM((NSA_KV_HEADS, rows, 128), f32)],
        compiler_params=_params(("parallel",)),
        name="win_attn",
    )(qn, gates, k, v, d0, d1, d2)


def _diff_lambda(dl, lam_init):
    return (jnp.exp(jnp.sum(dl[0:1] * dl[1:2], axis=1, keepdims=True))
            - jnp.exp(jnp.sum(dl[2:3] * dl[3:4], axis=1, keepdims=True)) + lam_init)


def _subln(od, lam_init, sg):
    y = od * lax.rsqrt(jnp.sum(od * od, axis=1, keepdims=True) * (1.0 / DIFF_V) + EPS) * sg
    return y * (1.0 - lam_init)


def _diff_body(lam_init, qd_ref, k_ref, v_ref, d0_ref, d1_ref, far_ref, dl_ref, sg_ref, od_ref, q_ref, m_ref, acc_ref):
    i = pl.program_id(0)
    lane = _lane(TQ)
    lam = _diff_lambda(dl_ref[...], lam_init)
    n_far = jnp.maximum(i - 1, 0)
    j1 = jnp.maximum(i - 1, 0)
    for h in range(DIFF_HEADS):
        qc = qd_ref[:, h * 128:(h + 1) * 128]
        o = (h % 2) * 64
        zero = jnp.zeros_like(qc)
        q_ref[h] = jnp.concatenate([jnp.where((lane >= o) & (lane < o + DIFF_QK), qc, zero),
                                    jnp.where((lane >= o + DIFF_QK) & (lane < o + 64), qc, zero)], axis=0)
    m_ref[...] = jnp.full(m_ref.shape, M_INIT, f32)
    acc_ref[...] = jnp.zeros(acc_ref.shape, f32)

    def logits(h, j):
        k0 = pl.multiple_of(j * TQ, TQ)
        return _dot_nt(q_ref[h], k_ref[pl.ds(k0, TQ), (h // 2) * 128:(h // 2 + 1) * 128])

    def near_logits(h):
        b1, b0 = d1_ref[h], d0_ref[h]
        s1 = jnp.where(i >= 1, logits(h, j1) + jnp.concatenate([b1, b1], axis=0), NEG_INF)
        s0 = logits(h, i) + jnp.concatenate([b0, b0], axis=0)
        return s1, s0

    def far_max(j, c):
        for h in range(DIFF_HEADS):
            s = logits(h, j)
            m_ref[h] = jnp.maximum(m_ref[h], jnp.maximum(s[:, :128], s[:, 128:]))
        return c

    lax.fori_loop(0, n_far, far_max, 0)
    for h in range(DIFF_HEADS):
        s1, s0 = near_logits(h)
        m = jnp.maximum(jnp.max(m_ref[h], axis=1, keepdims=True) + far_ref[0:1, h:h + 1],
                        jnp.maximum(jnp.max(s1, axis=1, keepdims=True), jnp.max(s0, axis=1, keepdims=True)))
        m_ref[h] = jnp.broadcast_to(m, (2 * TQ, 128))

    def accumulate(h, s, j):
        k0 = pl.multiple_of(j * TQ, TQ)
        p = jnp.exp2(s)
        acc_ref[h] += _dot(p.astype(bf16), v_ref[pl.ds(k0, TQ), h * 128:(h + 1) * 128])

    def far_acc(j, c):
        for h in range(DIFF_HEADS):
            shift = m_ref[h] - far_ref[0:1, h:h + 1]
            accumulate(h, logits(h, j) - _lane_tile(shift, TQ // 128), j)
        return c

    lax.fori_loop(0, n_far, far_acc, 0)
    for h in range(DIFF_HEADS):
        s1, s0 = near_logits(h)
        m = _lane_tile(m_ref[h], TQ // 128)
        accumulate(h, s1 - m, j1)
        accumulate(h, s0 - m, i)
    for h in range(DIFF_HEADS):
        o = _norm_low(acc_ref[h])
        od_ref[:, h * 128:(h + 1) * 128] = _subln(o[:TQ] - lam * o[TQ:], lam_init, sg_ref[...])


def _diff_attn(qd, k, v, d0, d1, far, dl, sg, lam_init):
    t = qd.shape[0]
    tile_spec = pl.BlockSpec((DIFF_HEADS, TQ, TQ), lambda i: (0, 0, 0))
    return pl.pallas_call(
        functools.partial(_diff_body, lam_init),
        grid=(t // TQ,),
        in_specs=[pl.BlockSpec((TQ, 512), lambda i: (i, 0)),
                  _resident((t, 256)), _resident((t, 512)),
                  tile_spec, tile_spec,
                  pl.BlockSpec((8, 128), lambda i: (0, 0)),
                  pl.BlockSpec((4, DIFF_QK), lambda i: (0, 0)),
                  pl.BlockSpec((1, 128), lambda i: (0, 0))],
        out_specs=pl.BlockSpec((TQ, 512), lambda i: (i, 0)),
        out_shape=jax.ShapeDtypeStruct((t, 512), f32),
        scratch_shapes=[pltpu.VMEM((DIFF_HEADS, 2 * TQ, 128), bf16),
                        pltpu.VMEM((DIFF_HEADS, 2 * TQ, 128), f32),
                        pltpu.VMEM((DIFF_HEADS, 2 * TQ, 128), f32)],
        compiler_params=_params(("parallel",)),
        name="diff_attn",
    )(qd, k, v, d0, d1, far, dl, sg)


def _rows48(ref):
    x = ref[0].astype(f32)
    return jnp.concatenate([x[:, h * 128:(h + 1) * 128] for h in range(NSA_HEADS)], axis=0)


def _pad_keys(new):
    return jnp.concatenate([new, jnp.zeros((PAGE - new.shape[0], new.shape[1]), f32)], axis=0)


def _sel_dec_body(*refs):
    pages = refs[1:1 + DEC_PAGES]
    (qn_ref, mneg_ref, gate_ref, new_ref, z_ref, blast_ref, bnew_ref, far_ref, os_ref,
     m_ref, l_ref, acc_ref) = refs[1 + DEC_PAGES:]
    g = pl.program_id(1)
    last = pl.num_programs(1) - 1

    @pl.when(g == 0)
    def _():
        _init_state(m_ref, acc_ref, l_ref)

    q = _rows48(qn_ref).astype(bf16)
    mn = mneg_ref[0].astype(f32)
    mrows = jnp.concatenate([mn[:, (h // NSA_GROUP) * 256:(h // NSA_GROUP + 1) * 256] for h in range(NSA_HEADS)],
                            axis=0).astype(bf16)
    k_t = jnp.concatenate([p[0, 0] for p in pages], axis=1).astype(bf16)
    v_t = jnp.concatenate([p[0, 1] for p in pages], axis=1).astype(bf16)
    blocks_per_step = DEC_KEYS // SEL_BLOCK
    z = z_ref[pl.ds(pl.multiple_of(N_BLK_LANES - g * blocks_per_step, blocks_per_step), N_BLK_LANES), :]
    cb = jnp.concatenate([jnp.broadcast_to(far_ref[0:1, h:h + 1], (8, DEC_KEYS)) for h in range(NSA_HEADS)], axis=0)
    s = _dot(q, k_t) + _dot(mrows, z) + jnp.where(g == last, blast_ref[...], cb)
    _flash_tile(s, 0.0, v_t, m_ref, acc_ref, 0, l_ref, v_transposed=True)

    @pl.when(g == last)
    def _():
        kn = _pad_keys(new_ref[0]).astype(bf16)
        _flash_tile(_dot_nt(q, kn[:, 0:128]) + bnew_ref[...], 0.0, kn[:, 128:256], m_ref, acc_ref, 0, l_ref)
        o = acc_ref[0] / jnp.maximum(l_ref[0], TINY)
        low = _lane(8) < 64
        for h in range(NSA_HEADS):
            oh = o[h * 8:(h + 1) * 8]
            if h // NSA_GROUP == 1:
                oh = pltpu.roll(oh, 64, 1)
            os_ref[0, :, h * 128:(h + 1) * 128] = jnp.where(low, oh * gate_ref[0, :, h * 3 + 1:h * 3 + 2], 0.0)


def _paged_specs(page_shape):
    def page_spec(i):
        return pl.BlockSpec((1,) + page_shape, lambda bb, g, tab: (tab[bb, g * DEC_PAGES + i], 0, 0, 0))
    return [page_spec(i) for i in range(DEC_PAGES)]


def _sel_dec(pool, table, qn, mneg, gates, new, zbig, blast, bnew, far):
    b, n_pages = table.shape
    c3 = lambda wd: pl.BlockSpec((1, 8, wd), lambda bb, g, tab: (bb, 0, 0))
    full = lambda a: pl.BlockSpec(a.shape, lambda bb, g, tab: (0,) * a.ndim)
    gs = pltpu.PrefetchScalarGridSpec(
        num_scalar_prefetch=1,
        grid=(b, n_pages // DEC_PAGES),
        in_specs=_paged_specs((2, 128, PAGE)) + [c3(768), c3(512), c3(128), c3(256), full(zbig), full(blast),
                                                 full(bnew), full(far)],
        out_specs=c3(768),
        scratch_shapes=[pltpu.VMEM((1, 48, 128), f32), pltpu.VMEM((1, 48, 128), f32), pltpu.VMEM((1, 48, 128), f32)],
    )
    return pl.pallas_call(
        _sel_dec_body,
        grid_spec=gs,
        out_shape=jax.ShapeDtypeStruct((b, 8, 768), f32),
        compiler_params=_params(("parallel", "arbitrary")),
        name="sel_dec",
    )(table, *([pool] * DEC_PAGES), qn, mneg, gates, new, zbig, blast, bnew, far)


def _win_dec_body(qn_ref, gate_ref, past_ref, new_ref, bias_ref, ow_ref, keep_ref):
    q = _rows48(qn_ref).astype(bf16)
    past = past_ref[0]
    n_past = past.shape[0]
    kv = jnp.concatenate([past, _pad_keys(new_ref[0])], axis=0).astype(bf16)
    s = _dot_nt(q, kv[:, 0:128]) + bias_ref[...]
    p = jnp.exp2(s - jnp.max(s, axis=1, keepdims=True))
    o = _dot(p.astype(bf16), kv[:, 128:256]) / jnp.maximum(jnp.sum(p, axis=1, keepdims=True), TINY)
    low = _lane(8) < 64
    for h in range(NSA_HEADS):
        oh = o[h * 8:(h + 1) * 8]
        if h // NSA_GROUP == 1:
            oh = pltpu.roll(oh, 64, 1)
        ow_ref[0, :, h * 128:(h + 1) * 128] = jnp.where(low, oh * gate_ref[0, :, h * 3 + 2:h * 3 + 3], 0.0)
    keep_ref[0, 0:n_past - 8, :] = past[8:, :]
    keep_ref[0, n_past - 8:n_past, :] = new_ref[0]


def _win_dec(qn, gates, past, new, bias):
    b, n_past, _ = past.shape
    c3 = lambda wd: pl.BlockSpec((1, 8, wd), lambda bb: (bb, 0, 0))
    return pl.pallas_call(
        _win_dec_body,
        grid=(b,),
        in_specs=[c3(768), c3(128), pl.BlockSpec((1, n_past, 256), lambda bb: (bb, 0, 0)), c3(256),
                  pl.BlockSpec(bias.shape, lambda bb: (0, 0))],
        out_specs=[c3(768), pl.BlockSpec((1, n_past, 256), lambda bb: (bb, 0, 0))],
        out_shape=[jax.ShapeDtypeStruct((b, 8, 768), f32), jax.ShapeDtypeStruct((b, n_past, 256), f32)],
        compiler_params=_params(("parallel",)),
        name="win_dec",
    )(qn, gates, past, new, bias)


def _diff_dec_body(lam_init, *refs):
    pages = refs[1:1 + DEC_PAGES]
    (qd_ref, new_ref, blast_ref, bnew_ref, far_ref, dl_ref, sg_ref, od_ref, m_ref, l_ref, acc_ref) = refs[1 + DEC_PAGES:]
    g = pl.program_id(1)
    last = pl.num_programs(1) - 1

    @pl.when(g == 0)
    def _():
        _init_state(m_ref, acc_ref, l_ref)

    lane = _lane(8)
    x = qd_ref[0].astype(f32)
    qs = []
    for h in range(DIFF_HEADS):
        qc = x[:, h * 128:(h + 1) * 128]
        if h % 2 == 1:
            qc = pltpu.roll(qc, 64, 1)
        qs.append(jnp.concatenate([jnp.where(lane < DIFF_QK, qc, 0.0),
                                   jnp.where((lane >= DIFF_QK) & (lane < 2 * DIFF_QK), qc, 0.0)], axis=0).astype(bf16))
    for h in range(DIFF_HEADS):
        kv = jnp.concatenate([p[0, :, h, :] for p in pages], axis=0).astype(bf16)
        bias = jnp.where(g == last, blast_ref[h * 16:(h + 1) * 16, :], far_ref[0:1, h:h + 1])
        _flash_tile(_dot_nt(qs[h], kv) + bias, 0.0, kv, m_ref, acc_ref, h, l_ref)

    @pl.when(g == last)
    def _():
        lam = _diff_lambda(dl_ref[...], lam_init)
        for h in range(DIFF_HEADS):
            kn = _pad_keys(new_ref[0, :, h * 128:(h + 1) * 128]).astype(bf16)
            _flash_tile(_dot_nt(qs[h], kn) + bnew_ref[h * 16:(h + 1) * 16, :], 0.0, kn, m_ref, acc_ref, h, l_ref)
            a = acc_ref[h] / jnp.maximum(l_ref[h], TINY)
            od = pltpu.roll(a[0:8] - lam * a[8:16], 64, 1)
            od_ref[0, :, h * 128:(h + 1) * 128] = _subln(jnp.where(lane < 64, od, 0.0), lam_init, sg_ref[...])


def _diff_dec(pool, table, qd, new, blast, bnew, far, dl, sg, lam_init):
    b, n_pages = table.shape
    c3 = lambda wd: pl.BlockSpec((1, 8, wd), lambda bb, g, tab: (bb, 0, 0))
    full = lambda a: pl.BlockSpec(a.shape, lambda bb, g, tab: (0,) * a.ndim)
    gs = pltpu.PrefetchScalarGridSpec(
        num_scalar_prefetch=1,
        grid=(b, n_pages // DEC_PAGES),
        in_specs=_paged_specs((PAGE, DIFF_HEADS, DIFF_ROW)) + [c3(512), c3(512), full(blast), full(bnew), full(far),
                                                               full(dl), full(sg)],
        out_specs=c3(512),
        scratch_shapes=[pltpu.VMEM((DIFF_HEADS, 16, 128), f32), pltpu.VMEM((DIFF_HEADS, 16, 128), f32),
                        pltpu.VMEM((DIFF_HEADS, 16, 128), f32)],
    )
    return pl.pallas_call(
        functools.partial(_diff_dec_body, lam_init),
        grid_spec=gs,
        out_shape=jax.ShapeDtypeStruct((b, 8, 512), f32),
        compiler_params=_params(("parallel", "arbitrary")),
        name="diff_dec",
    )(table, *([pool] * DEC_PAGES), qd, new, blast, bnew, far, dl, sg)


LRU_TT = 256


def _lru_body(tt, x_ref, gi_ref, cbuf_ref, h0_ref, cw_ref, cb_ref, wa_ref, ba_ref, wx_ref, bx_ref, lam_ref,
              y_ref, hl_ref, cn_ref, xe_ref, a_ref, b_ref, h_ref):
    j = pl.program_id(1)

    @pl.when(j == 0)
    def _():
        xe_ref[0:8, :] = jnp.zeros((8, LRU_WIDTH), f32)
        xe_ref[5:8, :] = cbuf_ref[0]
        h_ref[0:1, :] = h0_ref[0]

    x = x_ref[0]
    xe_ref[8:8 + tt, :] = x
    xc = cb_ref[...] + sum(xe_ref[5 + k:5 + k + tt, :] * cw_ref[k:k + 1, :] for k in range(CONV_W))
    xcb = xc.astype(bf16)
    r = _sigmoid(_dot(xcb, wa_ref[...]) + ba_ref[...])
    ig = _sigmoid(_dot(xcb, wx_ref[...]) + bx_ref[...])
    lam = lam_ref[...]
    softplus = jnp.maximum(-lam, 0.0) + jnp.log(1.0 + jnp.exp(-jnp.abs(lam)))
    log_a = -LRU_C * r * softplus
    a = jnp.exp(log_a)
    a_ref[...] = a
    b_ref[...] = jnp.sqrt(1.0 - jnp.exp(2.0 * log_a)) * (ig * xc)

    def step(k, h):
        h = a_ref[pl.ds(k, 1), :] * h + b_ref[pl.ds(k, 1), :]
        b_ref[pl.ds(k, 1), :] = h
        return h

    h = lax.fori_loop(0, tt, step, h_ref[0:1, :])
    h_ref[0:1, :] = h
    y_ref[0] = b_ref[...] * _gelu(gi_ref[0])
    hl_ref[0] = h
    cn_ref[0] = xe_ref[5 + tt:8 + tt, :]
    xe_ref[5:8, :] = xe_ref[5 + tt:8 + tt, :]


def _rglru(x, gate_in, conv_buf, h0, cw, cb, wa, ba, wx, bx, lam):
    b, t, w = x.shape
    tt = min(LRU_TT, t)

    def bd(wblk):
        m = jnp.zeros((w, w), f32)
        for n in range(LRU_BLOCKS):
            m = m.at[n * LRU_BLOCK:(n + 1) * LRU_BLOCK, n * LRU_BLOCK:(n + 1) * LRU_BLOCK].set(wblk[n])
        return m.astype(bf16)

    vec = lambda: pl.BlockSpec((1, w), lambda bb, j: (0, 0))
    return pl.pallas_call(
        functools.partial(_lru_body, tt),
        grid=(b, t // tt),
        in_specs=[pl.BlockSpec((1, tt, w), lambda bb, j: (bb, j, 0)),
                  pl.BlockSpec((1, tt, w), lambda bb, j: (bb, j, 0)),
                  pl.BlockSpec((1, 3, w), lambda bb, j: (bb, 0, 0)),
                  pl.BlockSpec((1, 1, w), lambda bb, j: (bb, 0, 0)),
                  pl.BlockSpec((CONV_W, w), lambda bb, j: (0, 0)), vec(),
                  pl.BlockSpec((w, w), lambda bb, j: (0, 0)), vec(),
                  pl.BlockSpec((w, w), lambda bb, j: (0, 0)), vec(), vec()],
        out_specs=[pl.BlockSpec((1, tt, w), lambda bb, j: (bb, j, 0)),
                   pl.BlockSpec((1, 1, w), lambda bb, j: (bb, 0, 0)),
                   pl.BlockSpec((1, 3, w), lambda bb, j: (bb, 0, 0))],
        out_shape=[jax.ShapeDtypeStruct((b, t, w), f32), jax.ShapeDtypeStruct((b, 1, w), f32),
                   jax.ShapeDtypeStruct((b, 3, w), f32)],
        scratch_shapes=[pltpu.VMEM((tt + 8, w), f32), pltpu.VMEM((tt, w), f32), pltpu.VMEM((tt, w), f32),
                        pltpu.VMEM((8, w), f32)],
        compiler_params=_params(("parallel", "arbitrary")),
        name="rglru",
    )(x, gate_in, conv_buf, h0.reshape(b, 1, w), cw, cb.reshape(1, w), bd(wa), ba.reshape(1, w), bd(wx),
      bx.reshape(1, w), lam.reshape(1, w))


def _outproj_body(x_ref, gt_ref, oc_ref, os_ref, ow_ref, od_ref, yr_ref, w_ref, o_ref):
    on = oc_ref[...] + os_ref[...] + ow_ref[...]
    cat = jnp.concatenate([on.astype(bf16), od_ref[...].astype(bf16), yr_ref[...].astype(bf16)], axis=1)
    o_ref[...] = x_ref[...] + gt_ref[...] * _dot(cat, w_ref[...])


def _outproj(x, gt, oc, os_, ow, od, yr, w, tm):
    m = x.shape[0]
    row = lambda wd: pl.BlockSpec((tm, wd), lambda i: (i, 0))
    return pl.pallas_call(
        _outproj_body,
        grid=(m // tm,),
        in_specs=[row(D_MODEL), _rowspec(gt.shape[0], tm, D_MODEL), row(768), row(768), row(768), row(512),
                  row(384), pl.BlockSpec((N_OUT_IN, D_MODEL), lambda i: (0, 0))],
        out_specs=row(D_MODEL),
        out_shape=jax.ShapeDtypeStruct((m, D_MODEL), f32),
        compiler_params=_params(("parallel",)),
        name="outproj",
    )(x, gt, oc, os_, ow, od, yr, w)


def _ffn_body(n_exp, final, x_ref, g_ref, sc_ref, sh_ref, gt_ref, rw_ref, rb_ref, w1_ref, w3_ref, w2_ref, fg_ref,
              o_ref, h_ref, acc_ref, gate_ref):
    e, f = pl.program_id(1), pl.program_id(2)

    @pl.when((e == 0) & (f == 0))
    def _():
        h = _normmod(x_ref[...], g_ref[...], sc_ref[...], sh_ref[...])
        h_ref[...] = h.astype(bf16)
        acc_ref[...] = jnp.zeros_like(acc_ref)
        if n_exp > 1:
            logits = jnp.dot(h, rw_ref[...], preferred_element_type=f32, precision=lax.Precision.HIGHEST)
            lane = lax.broadcasted_iota(jnp.int32, logits.shape, 1)
            logits = jnp.where(lane < n_exp, logits + rb_ref[...], NEG_INF)
            v1 = jnp.max(logits, axis=1, keepdims=True)
            i1 = jnp.min(jnp.where(logits == v1, lane, 4096), axis=1, keepdims=True)
            rest = jnp.where(lane == i1, NEG_INF, logits)
            v2 = jnp.max(rest, axis=1, keepdims=True)
            i2 = jnp.min(jnp.where(rest == v2, lane, 4096), axis=1, keepdims=True)
            e2 = jnp.exp(v2 - v1)
            w_1 = 1.0 / (1.0 + e2)
            w_2 = e2 / (1.0 + e2)
            gate = jnp.where(lane == i1, w_1, 0.0) + jnp.where(lane == i2, w_2, 0.0)
            for k in range(n_exp):
                gate_ref[k] = gate[:, k:k + 1]

    hb = h_ref[...]
    hid = _silu(_dot(hb, w1_ref[0])) * _dot(hb, w3_ref[0])
    if n_exp > 1:
        hid = hid * gate_ref[e]
    acc_ref[...] += _dot(hid.astype(bf16), w2_ref[0])

    @pl.when((e == n_exp - 1) & (f == pl.num_programs(2) - 1))
    def _():
        y = x_ref[...] + gt_ref[...] * acc_ref[...]
        if final:
            y = y * lax.rsqrt(jnp.mean(y * y, axis=-1, keepdims=True) + EPS) * fg_ref[...]
        o_ref[...] = y


def _ffn(x, g, sc, sh, gt, rw, rb, w1, w3, w2, fg, final, tm, tf):
    m = x.shape[0]
    n_exp, _, ff = w1.shape
    vec = lambda: pl.BlockSpec((1, D_MODEL), lambda i, e, f: (0, 0))

    def rowspec(a):
        if a.shape[0] == 1:
            return vec()
        return pl.BlockSpec((tm, D_MODEL), lambda i, e, f: (i, 0))

    return pl.pallas_call(
        functools.partial(_ffn_body, n_exp, final),
        grid=(m // tm, n_exp, ff // tf),
        in_specs=[pl.BlockSpec((tm, D_MODEL), lambda i, e, f: (i, 0)), vec(), rowspec(sc), rowspec(sh), rowspec(gt),
                  pl.BlockSpec((D_MODEL, 128), lambda i, e, f: (0, 0)),
                  pl.BlockSpec((1, 128), lambda i, e, f: (0, 0)),
                  pl.BlockSpec((1, D_MODEL, tf), lambda i, e, f: (e, 0, f)),
                  pl.BlockSpec((1, D_MODEL, tf), lambda i, e, f: (e, 0, f)),
                  pl.BlockSpec((1, tf, D_MODEL), lambda i, e, f: (e, f, 0)),
                  vec()],
        out_specs=pl.BlockSpec((tm, D_MODEL), lambda i, e, f: (i, 0)),
        out_shape=jax.ShapeDtypeStruct((m, D_MODEL), f32),
        scratch_shapes=[pltpu.VMEM((tm, D_MODEL), bf16), pltpu.VMEM((tm, D_MODEL), f32),
                        pltpu.VMEM((N_EXPERTS, tm, 1), f32)],
        compiler_params=_params(("parallel", "arbitrary", "arbitrary")),
        name="ffn",
    )(x, g.reshape(1, -1), sc, sh, gt, rw, rb, w1, w3, w2, fg.reshape(1, -1))


def _t5_bucket(dist):
    n = jnp.maximum(dist, 0)
    exact = N_BUCKETS // 2
    nf = jnp.maximum(n, 1).astype(f32)
    large = exact + (jnp.log(nf / exact) / math.log(MAX_DISTANCE / exact) * (N_BUCKETS - exact)).astype(jnp.int32)
    return jnp.where(n < exact, n, jnp.minimum(large, N_BUCKETS - 1))


def _by_dist(bd, dist, valid):
    onehot = jax.nn.one_hot(jnp.clip(dist, 0, 127), 128, dtype=f32)
    v = jnp.einsum('...d,dh->h...', onehot, bd, precision=lax.Precision.HIGHEST)
    return jnp.where(valid[None], v, NEG_INF)


def _bias_tables(rel_bias, pos0_dec):
    nh = rel_bias.shape[1]
    bd = rel_bias[_t5_bucket(jnp.arange(128, dtype=jnp.int32))] * LOG2E
    far = rel_bias[N_BUCKETS - 1] * LOG2E
    r = jnp.arange(TQ, dtype=jnp.int32)[:, None]
    c = jnp.arange(TQ, dtype=jnp.int32)[None, :]
    d0 = _by_dist(bd, r - c, r - c >= 0)
    d1 = _by_dist(bd, r - c + TQ, r - c + TQ >= 0)
    d2 = jnp.where((c > r)[None], far[:, None, None], NEG_INF)
    bd_pad = jnp.zeros((16, 128), f32).at[:nh].set(bd.T)
    far_pad = jnp.zeros((8, 128), f32).at[0, :nh].set(far)
    r8 = jnp.arange(8, dtype=jnp.int32)[:, None]
    ck = jnp.arange(DEC_KEYS, dtype=jnp.int32)[None, :]
    last = _by_dist(bd, DEC_KEYS + r8 - ck, jnp.ones((8, DEC_KEYS), bool))
    cn = jnp.arange(PAGE, dtype=jnp.int32)[None, :]
    new = _by_dist(bd, r8 - cn, (r8 - cn >= 0) & (cn < 8))
    n_win = min(WINDOW, pos0_dec)
    cw = jnp.arange(n_win, dtype=jnp.int32)[None, :]
    dw = n_win + r8 - cw
    win = _by_dist(bd, dw, dw < WINDOW)
    return d0, d1, d2, bd_pad, far_pad, far, last, new, win


def _m5(n_cmp):
    n = np.arange(n_cmp)[:, None]
    j = np.arange(N_BLK_LANES)[None, :]
    return jnp.asarray(((n >= 4 * j) & (n <= 4 * j + 4)).astype(np.float32), bf16)


def _zbig():
    rho = np.arange(2 * N_BLK_LANES)[:, None]
    c = np.arange(DEC_KEYS)[None, :]
    return jnp.asarray((rho == N_BLK_LANES + c // SEL_BLOCK).astype(np.float32), bf16)


def _split6(mod):
    return [mod[:, k * D_MODEL:(k + 1) * D_MODEL] for k in range(6)]


def _prompt_layer(x, mod, lw, tb, l, final_g, last):
    t = x.shape[0]
    sh1, sc1, gt1, sh2, sc2, gt2 = _split6(mod)
    (qn, qd, gates, cmp, sel, win, diff, xr, gr, kaug, vsel, kwin, vwin, kdiff, vdiff) = _proj(
        x, lw['norm_mix_g'], sc1, sh1, lw['w_aug'], lw['onerow'], 256)
    n_pages = t // PAGE
    ident = jnp.arange(n_pages, dtype=jnp.int32).reshape(1, n_pages)
    slab = _compress(cmp.reshape(n_pages, 8, 4096), ident, *lw['cmp'])
    oc, mneg = _cmp_attn(qn[None], gates[None], slab, tb['bd_nsa'], tb['m5_p'], TQ, 0)
    os_ = _sel_attn(qn, mneg[0], gates, kaug, vsel, tb['d0n'], tb['d1n'], tb['far_n'])
    ow = _win_attn(qn, gates, kwin, vwin, tb['d0n'], tb['d1n'], tb['d2n'])
    lam_init = 0.8 - 0.6 * math.exp(-0.3 * l)
    od = _diff_attn(qd, kdiff, vdiff, tb['d0d'], tb['d1d'], tb['far_d'], lw['diff_lambda'], lw['subln'], lam_init)
    zeros_c = jnp.zeros((1, CONV_W - 1, LRU_WIDTH), f32)
    zeros_h = jnp.zeros((1, LRU_WIDTH), f32)
    yr, hl, cn = _rglru(xr[None], gr[None], zeros_c, zeros_h, *lw['lru'])
    x = _outproj(x, gt1, oc[0], os_, ow, od, yr[0], lw['w_out'], 512)
    x = _ffn(x, lw['norm_ffn_g'], sc2, sh2, gt2, *lw['ffn'], final_g, last, 512, 1408)
    n_keep = min(WINDOW, t)
    state = (cmp.reshape(1, t, 2, NSA_KV_HEADS, HEAD_DIM), sel.reshape(1, t, 2, NSA_KV_HEADS, HEAD_DIM),
             diff.reshape(1, t, DIFF_HEADS, DIFF_ROW), win[t - n_keep:].reshape(1, n_keep, 2, NSA_KV_HEADS, HEAD_DIM),
             hl.reshape(1, LRU_WIDTH), cn)
    return x, state


def _sample_layer(x, mod, lw, tb, l, final_g, last, caches, page_table, nb, dec):
    pool_cmp, pool_sel, pool_diff, win_past, h0, conv0 = caches
    pos0 = page_table.shape[1] * PAGE
    rep = lambda a: jnp.repeat(a, dec, axis=0)
    sh1, sc1, gt1, sh2, sc2, gt2 = [rep(m) for m in _split6(mod)]
    m = nb * dec
    (qn, qd, gates, cmp, sel, win, diff, xr, gr, _, _, _, _, _, _) = _proj(
        x, lw['norm_mix_g'], sc1, sh1, lw['w_aug'], lw['onerow'], m)
    b3 = lambda a: a.reshape(nb, dec, a.shape[-1])
    slab = _compress(pool_cmp, page_table, *lw['cmp'])
    oc, mneg = _cmp_attn(b3(qn), b3(gates), slab, tb['bd_nsa'], tb['m5_s'], dec, pos0)
    os_ = _sel_dec(pool_sel, page_table, b3(qn), mneg, b3(gates), b3(sel), tb['zbig'], tb['last_n'], tb['new_n'],
                   tb['far_n'])
    ow, win_keep = _win_dec(b3(qn), b3(gates), win_past, b3(win), tb['win_n'])
    lam_init = 0.8 - 0.6 * math.exp(-0.3 * l)
    od = _diff_dec(pool_diff, page_table, b3(qd), b3(diff), tb['last_d'], tb['new_d'], tb['far_d'],
                   lw['diff_lambda'], lw['subln'], lam_init)
    yr, hl, cn = _rglru(b3(xr), b3(gr), conv0, h0, *lw['lru'])
    flat = lambda a: a.reshape(m, a.shape[-1])
    x = _outproj(x, gt1, flat(oc), flat(os_), flat(ow), flat(od), flat(yr), lw['w_out'], m)
    x = _ffn(x, lw['norm_ffn_g'], sc2, sh2, gt2, *lw['ffn'], final_g, last, m, 1408)
    n_keep = win_keep.shape[1]
    state = (cmp.reshape(nb, dec, 2, NSA_KV_HEADS, HEAD_DIM), sel.reshape(nb, dec, 2, NSA_KV_HEADS, HEAD_DIM),
             diff.reshape(nb, dec, DIFF_HEADS, DIFF_ROW), win_keep.reshape(nb, n_keep, 2, NSA_KV_HEADS, HEAD_DIM),
             hl.reshape(nb, LRU_WIDTH), cn)
    return x, state


def kernel(x_prompt, x_sample, c_prompt, c_sample, cache_nsa_cmp, cache_nsa_sel, cache_diff, cache_nsa_win, state_lru_h, state_lru_conv, page_table, rel_bias, norm_mix_g, norm_ffn_g, final_norm_g, w_ada, b_ada, w_in, cmp_pe, cmp_w1, cmp_b1, cmp_w2, diff_lambda, diff_subln_g, lru_conv_w, lru_conv_b, lru_wa, lru_ba, lru_wx, lru_bx, lru_lambda, w_out, ffn_w1, ffn_w3, ffn_w2, router_w, router_b, moe_w1, moe_w3, moe_w2):
    depth = w_in.shape[0]
    t = x_prompt.shape[1]
    nb, dec = x_sample.shape[0], x_sample.shape[1]
    n_pool = cache_nsa_cmp.shape[1]
    pos0_dec = page_table.shape[1] * PAGE
    cols, onerow = _proj_cols()
    rows = _outproj_rows()
    onerow = jnp.asarray(onerow)
    d0, d1, d2, bd, far_pad, far, last, new, win = _bias_tables(rel_bias, pos0_dec)
    nh = NSA_HEADS
    stack_rows = lambda a: a.reshape(-1, a.shape[-1])
    twice = lambda a: jnp.repeat(a, 2, axis=0)
    tb = {'d0n': d0[:nh], 'd1n': d1[:nh], 'd2n': d2[:nh], 'd0d': d0[nh:], 'd1d': d1[nh:], 'bd_nsa': bd,
          'far_n': far_pad, 'far_d': jnp.zeros((8, 128), f32).at[0, :DIFF_HEADS].set(far[nh:]),
          'm5_p': _m5(t // CMP_STRIDE), 'm5_s': _m5(pos0_dec // CMP_STRIDE), 'zbig': _zbig(),
          'last_n': stack_rows(last[:nh]), 'new_n': stack_rows(new[:nh]),
          'win_n': jnp.concatenate([stack_rows(win[:nh]), stack_rows(new[:nh])], axis=1),
          'last_d': stack_rows(twice(last[nh:])), 'new_d': stack_rows(twice(new[nh:]))}

    c_all = jnp.concatenate([c_prompt, c_sample, jnp.zeros((7, D_MODEL), f32)], axis=0)
    layers = []
    for l in range(depth):
        j = l // 2
        lw = {'norm_mix_g': norm_mix_g[l], 'norm_ffn_g': norm_ffn_g[l],
              'w_aug': _take_cols(w_in[l], cols).astype(bf16), 'onerow': onerow,
              'cmp': _compress_weights(cmp_pe[l], cmp_w1[l], cmp_b1[l], cmp_w2[l]),
              'diff_lambda': diff_lambda[l],
              'subln': jnp.zeros((1, 128), f32).at[0, :64].set(diff_subln_g[l]),
              'lru': (lru_conv_w[l], lru_conv_b[l], lru_wa[l], lru_ba[l], lru_wx[l], lru_bx[l], lru_lambda[l]),
              'w_out': _take_rows(w_out[l], rows).astype(bf16)}
        if l % 2 == 0:
            lw['ffn'] = (jnp.zeros((D_MODEL, 128), f32), jnp.zeros((1, 128), f32),
                         ffn_w1[j][None].astype(bf16), ffn_w3[j][None].astype(bf16), ffn_w2[j][None].astype(bf16))
        else:
            rw = jnp.zeros((D_MODEL, 128), f32).at[:, :N_EXPERTS].set(router_w[j])
            rb = jnp.zeros((1, 128), f32).at[0, :N_EXPERTS].set(router_b[j])
            lw['ffn'] = (rw, rb, moe_w1[j].astype(bf16), moe_w3[j].astype(bf16), moe_w2[j].astype(bf16))
        layers.append(lw)

    mods = [_ada(c_all, w_ada[l], b_ada[l]) for l in range(depth)]

    x = x_prompt[0]
    p_states = []
    for l in range(depth):
        x, st = _prompt_layer(x, mods[l][0:1], layers[l], tb, l, final_norm_g, l == depth - 1)
        p_states.append(st)
    y_prompt = x[None]

    xs = x_sample.reshape(nb * dec, D_MODEL)
    s_states = []
    pool_cmp = cache_nsa_cmp.reshape(depth * n_pool, 8, 4096)
    pool_sel = jnp.transpose(cache_nsa_sel, (0, 1, 3, 4, 5, 2)).reshape(depth * n_pool, 2, 128, PAGE)
    pool_diff = cache_diff.reshape(depth * n_pool, PAGE, DIFF_HEADS, DIFF_ROW)
    for l in range(depth):
        caches = (pool_cmp, pool_sel, pool_diff, cache_nsa_win[l].reshape(nb, -1, 256),
                  state_lru_h[l], state_lru_conv[l])
        xs, st = _sample_layer(xs, mods[l][1:1 + nb], layers[l], tb, l, final_norm_g, l == depth - 1, caches,
                               page_table + l * n_pool, nb, dec)
        s_states.append(st)
    y_sample = xs.reshape(nb, dec, D_MODEL)

    stack = lambda sts, k: jnp.stack([s[k] for s in sts], axis=0)
    return (y_prompt, y_sample, *[stack(p_states, k) for k in range(6)], *[stack(s_states, k) for k in range(6)])
```

```python
import functools
import math

import numpy as np
import jax
import jax.numpy as jnp
from jax import lax
from jax.experimental import pallas as pl
from jax.experimental.pallas import tpu as pltpu

f32 = jnp.float32
bf16 = jnp.bfloat16

D_MODEL = 1024
HEAD_DIM = 64
NSA_HEADS = 6
NSA_KV_HEADS = 2
NSA_GROUP = 3
CMP_STRIDE = 16
CMP_LEN = 32
CMP_HIDDEN = 128
SEL_BLOCK = 64
SEL_RATIO = 4
SEL_TOPK = 16
WINDOW = 512
DIFF_HEADS = 4
DIFF_QK = 32
DIFF_V = 64
DIFF_ROW = 128
LRU_WIDTH = 384
LRU_BLOCKS = 6
LRU_BLOCK = 64
CONV_W = 4
LRU_C = 8.0
D_NSA = 384
D_DIFF = 256
N_BUCKETS = 32
MAX_DISTANCE = 128
D_FF = 2816
N_EXPERTS = 8
D_FF_EXPERT = 1408
EPS = 1e-6
NEG_INF = -1e30
TINY = 1e-30
FORCE_SCORE = 1e4
ATT_SCALE = HEAD_DIM ** -0.5
DIFF_SCALE = DIFF_QK ** -0.5
PAGE = 128
LOG2E = 1.4426950408889634

TQ = 256
N_BLK_LANES = 256
M_INIT = -1e20
VMEM_LIMIT = 56 * 1024 * 1024
DEC_PAGES = 16
DEC_KEYS = DEC_PAGES * PAGE

(O_QN, O_QD, O_GATE, O_CMP, O_SEL, O_WIN, O_DIFF, O_XR, O_GR, O_KAUG, O_VSEL, O_KWIN, O_VWIN, O_KDIFF, O_VDIFF,
 N_PROJ) = (0, 768, 1280, 1408, 1664, 1920, 2176, 2688, 3072, 3456, 3712, 3968, 4096, 4352, 4608, 5120)
S_QN, S_GN, S_CMP, S_SEL, S_WIN, S_QD, S_KD, S_VD, S_XR, S_GR, D_IN = (
    0, 384, 402, 658, 914, 1170, 1426, 1682, 1938, 2322, 2706)
N_OUT_IN = 768 + 512 + 384


def _proj_cols():
    c, one = [], []

    def seg(cols, ones=None):
        c.extend(cols)
        one.extend(ones if ones is not None else [0.0] * len(cols))

    for h in range(NSA_HEADS):
        hk = h // NSA_GROUP
        seg([-1] * (hk * 64) + [S_QN + h * 64 + d for d in range(64)] + [-1] * (64 - hk * 64))
    for h in range(DIFF_HEADS):
        o = (h % 2) * 64
        seg([-1] * o + [S_QD + h * 64 + d for d in range(64)] + [-1] * (64 - o))
    seg([S_GN + i for i in range(18)] + [-1] * 110)
    seg(list(range(S_CMP, S_CMP + 256)) + list(range(S_SEL, S_SEL + 256)) + list(range(S_WIN, S_WIN + 256)))
    for h in range(DIFF_HEADS):
        seg([S_KD + h * 64 + d for d in range(64)] + [S_VD + h * 64 + d for d in range(64)])
    seg(list(range(S_XR, S_XR + 384)) + list(range(S_GR, S_GR + 384)))
    seg([-1] * 128 + list(range(S_SEL, S_SEL + 128)))
    for hk in range(NSA_KV_HEADS):
        seg([S_SEL + 128 + hk * 64 + d for d in range(64)] + [-1] * 64, [0.0] * 64 + [1.0] * 64)
    seg(list(range(S_WIN, S_WIN + 128)))
    for hk in range(NSA_KV_HEADS):
        seg([S_WIN + 128 + hk * 64 + d for d in range(64)] + [-1] * 64, [0.0] * 64 + [1.0] * 64)
    seg(list(range(S_KD, S_KD + 256)))
    for h in range(DIFF_HEADS):
        seg([S_VD + h * 64 + d for d in range(64)] + [-1] * 64, [0.0] * 64 + [1.0] * 64)
    assert len(c) == N_PROJ
    return np.asarray(c, np.int32), np.asarray(one, np.float32).reshape(1, N_PROJ)


def _outproj_rows():
    r = []
    for h in range(NSA_HEADS):
        r += [h * 64 + d for d in range(64)] + [-1] * 64
    for h in range(DIFF_HEADS):
        r += [D_NSA + h * 64 + d for d in range(64)] + [-1] * 64
    r += list(range(D_NSA + D_DIFF, 1024))
    assert len(r) == N_OUT_IN
    return np.asarray(r, np.int32)


def _take_cols(w, cols):
    return jnp.where(cols[None, :] >= 0, jnp.take(w, np.maximum(cols, 0), axis=1), 0.0)


def _take_rows(w, rows):
    return jnp.where(rows[:, None] >= 0, jnp.take(w, np.maximum(rows, 0), axis=0), 0.0)


def _params(sem):
    return pltpu.CompilerParams(dimension_semantics=sem, vmem_limit_bytes=VMEM_LIMIT)


def _dot(a, b):
    return jnp.dot(a, b, preferred_element_type=f32)


def _dot_nt(a, b):
    return lax.dot_general(a, b, (((1,), (1,)), ((), ())), preferred_element_type=f32)


def _gelu(x):
    return 0.5 * x * (1.0 + jnp.tanh(math.sqrt(2.0 / math.pi) * (x + 0.044715 * (x * x * x))))


def _sigmoid(x):
    return 1.0 / (1.0 + jnp.exp(-x))


def _silu(x):
    return x * _sigmoid(x)


def _normmod(x, g, sc, sh):
    h = x * lax.rsqrt(jnp.mean(x * x, axis=-1, keepdims=True) + EPS) * g
    return h * (1.0 + sc) + sh


def _rowspec(rows, tm, width):
    if rows == 1:
        return pl.BlockSpec((1, width), lambda i: (0, 0))
    return pl.BlockSpec((tm, width), lambda i: (i, 0))


def _resident(shape):
    return pl.BlockSpec(shape, lambda i: (0,) * len(shape), pipeline_mode=pl.Buffered(1))


def _lane(rows):
    return lax.broadcasted_iota(jnp.int32, (rows, 128), 1)


def _ada_body(c_ref, w_ref, b_ref, o_ref):
    c = c_ref[...]
    o_ref[...] = _dot(_silu(c).astype(bf16), w_ref[...].astype(bf16)) + b_ref[...]


def _ada(c, w, b):
    m, n = c.shape[0], w.shape[1]
    tn = 1536
    return pl.pallas_call(
        _ada_body,
        grid=(n // tn,),
        in_specs=[pl.BlockSpec((m, D_MODEL), lambda j: (0, 0)),
                  pl.BlockSpec((D_MODEL, tn), lambda j: (0, j)),
                  pl.BlockSpec((1, tn), lambda j: (0, j))],
        out_specs=pl.BlockSpec((m, tn), lambda j: (0, j)),
        out_shape=jax.ShapeDtypeStruct((m, n), f32),
        compiler_params=_params(("parallel",)),
        name="ada",
    )(c, w, b.reshape(1, n))


PROJ_OUTS = [(768, bf16), (512, bf16), (128, f32), (256, f32), (256, f32), (256, f32), (512, f32), (384, f32),
             (384, f32), (256, bf16), (256, bf16), (128, bf16), (256, bf16), (256, bf16), (512, bf16)]


def _proj_body(tm, x_ref, g_ref, sc_ref, sh_ref, w_ref, one_ref, qn_ref, qd_ref, gate_ref, cmp_ref, sel_ref,
               win_ref, diff_ref, xr_ref, gr_ref, kaug_ref, vsel_ref, kwin_ref, vwin_ref, kdiff_ref, vdiff_ref):
    h = _normmod(x_ref[...], g_ref[...], sc_ref[...], sh_ref[...])
    pr = _dot(h.astype(bf16), w_ref[...]) + one_ref[...]
    qn_ref[...] = (pr[:, O_QN:O_QD] * (ATT_SCALE * LOG2E)).astype(bf16)
    qd_ref[...] = (pr[:, O_QD:O_GATE] * (DIFF_SCALE * LOG2E)).astype(bf16)
    gate_ref[...] = _sigmoid(pr[:, O_GATE:O_CMP])
    cmp_ref[...] = pr[:, O_CMP:O_SEL]
    sel_ref[...] = pr[:, O_SEL:O_WIN]
    win_ref[...] = pr[:, O_WIN:O_DIFF]
    diff_ref[...] = pr[:, O_DIFF:O_XR]
    xr_ref[...] = pr[:, O_XR:O_GR]
    gr_ref[...] = pr[:, O_GR:O_KAUG]
    t = pl.program_id(0) * tm + lax.broadcasted_iota(jnp.int32, (tm, 128), 0)
    onehot = (_lane(tm) == (t // SEL_BLOCK) % 128).astype(f32)
    kaug_ref[:, 0:128] = onehot.astype(bf16)
    kaug_ref[:, 128:256] = pr[:, O_KAUG + 128:O_VSEL].astype(bf16)
    vsel_ref[...] = pr[:, O_VSEL:O_KWIN].astype(bf16)
    kwin_ref[...] = pr[:, O_KWIN:O_VWIN].astype(bf16)
    vwin_ref[...] = pr[:, O_VWIN:O_KDIFF].astype(bf16)
    kdiff_ref[...] = pr[:, O_KDIFF:O_VDIFF].astype(bf16)
    vdiff_ref[...] = pr[:, O_VDIFF:N_PROJ].astype(bf16)


def _proj(x, g, sc, sh, w, onerow, tm):
    m = x.shape[0]
    return pl.pallas_call(
        functools.partial(_proj_body, tm),
        grid=(m // tm,),
        in_specs=[pl.BlockSpec((tm, D_MODEL), lambda i: (i, 0)),
                  pl.BlockSpec((1, D_MODEL), lambda i: (0, 0)),
                  _rowspec(sc.shape[0], tm, D_MODEL), _rowspec(sh.shape[0], tm, D_MODEL),
                  pl.BlockSpec((D_MODEL, N_PROJ), lambda i: (0, 0)),
                  pl.BlockSpec((1, N_PROJ), lambda i: (0, 0))],
        out_specs=[pl.BlockSpec((tm, wd), lambda i: (i, 0)) for wd, _ in PROJ_OUTS],
        out_shape=[jax.ShapeDtypeStruct((m, wd), dt) for wd, dt in PROJ_OUTS],
        compiler_params=_params(("parallel",)),
        name="proj",
    )(x, g.reshape(1, -1), sc, sh, w, onerow)


CMP_PAGES = 16
CMP_ROWS = CMP_PAGES * 8
CMP_OUT = 384


def _compress_body(*refs):
    pages = refs[1:1 + CMP_PAGES]
    wc_ref, pe_ref, b1_ref, w2_ref, o_ref, carry_ref = refs[1 + CMP_PAGES:]
    g = pl.program_id(1)

    @pl.when(g == 0)
    def _():
        carry_ref[...] = jnp.zeros_like(carry_ref)

    x = jnp.concatenate([p[0, 0] for p in pages], axis=0).astype(bf16)
    a = _dot(x, wc_ref[...])
    pe = _dot(pe_ref[...].astype(bf16), wc_ref[...])
    const = pe[0:1, :512] + pe[1:2, 512:] + b1_ref[...]
    first, second = a[:, :512], a[:, 512:]
    rid = lax.broadcasted_iota(jnp.int32, (CMP_ROWS, 512), 0)
    prev_first = jnp.where(rid == 0, carry_ref[0:1, :], pltpu.roll(first, 1, 0))
    carry_ref[0:1, :] = first[CMP_ROWS - 1:CMP_ROWS, :]
    hid = _gelu(prev_first + second + const)
    o_ref[0] = _dot(hid.astype(bf16), w2_ref[...]).astype(bf16)


def _compress(pool, table, wc, pe2, b1, w2bd):
    b, n_pages = table.shape
    steps = n_pages // CMP_PAGES

    def page_spec(i):
        return pl.BlockSpec((1, 1, 8, 4096), lambda bb, g, tab: (tab[bb, g * CMP_PAGES + i], 0, 0, 0))

    gs = pltpu.PrefetchScalarGridSpec(
        num_scalar_prefetch=1,
        grid=(b, steps),
        in_specs=[page_spec(i) for i in range(CMP_PAGES)] + [
            pl.BlockSpec((4096, 1024), lambda bb, g, tab: (0, 0)),
            pl.BlockSpec((8, 4096), lambda bb, g, tab: (0, 0)),
            pl.BlockSpec((1, 512), lambda bb, g, tab: (0, 0)),
            pl.BlockSpec((512, CMP_OUT), lambda bb, g, tab: (0, 0))],
        out_specs=pl.BlockSpec((1, CMP_ROWS, CMP_OUT), lambda bb, g, tab: (bb, g, 0)),
        scratch_shapes=[pltpu.VMEM((8, 512), f32)],
    )
    pool4 = pool.reshape(pool.shape[0], 1, 8, 4096)
    return pl.pallas_call(
        _compress_body,
        grid_spec=gs,
        out_shape=jax.ShapeDtypeStruct((b, n_pages * 8, CMP_OUT), bf16),
        compiler_params=_params(("parallel", "arbitrary")),
        name="compress",
    )(table, *([pool4] * CMP_PAGES), wc, pe2, b1, w2bd)


def _compress_weights(cmp_pe, cmp_w1, cmp_b1, cmp_w2):
    w1 = cmp_w1.reshape(2, 2, CMP_STRIDE, HEAD_DIM, CMP_HIDDEN)
    eye4 = jnp.eye(4, dtype=f32)
    w1s = jnp.stack([w1[0], w1[0], w1[1], w1[1]], axis=0)
    wc = jnp.einsum('sflde,st->lsdfte', w1s, eye4).reshape(4096, 1024)
    pe = cmp_pe.reshape(2, 2, CMP_STRIDE, HEAD_DIM)
    pes = jnp.stack([pe[0], pe[0], pe[1], pe[1]], axis=0)
    pe2 = jnp.transpose(pes, (1, 2, 0, 3)).reshape(2, 4096)
    pe2 = jnp.concatenate([pe2, jnp.zeros((6, 4096), f32)], axis=0)
    b1 = jnp.stack([cmp_b1[0], cmp_b1[0], cmp_b1[1], cmp_b1[1]], axis=0).reshape(1, 512)
    w2bd = jnp.zeros((512, CMP_OUT), f32)
    for s, lane0 in ((0, 0), (1, 64), (2, 128), (3, 256)):
        w2bd = w2bd.at[s * 128:(s + 1) * 128, lane0:lane0 + 64].set(cmp_w2[s // 2])
    return wc.astype(bf16), pe2, b1, w2bd.astype(bf16)


def _lane_tile(x, reps):
    return x if reps == 1 else jnp.concatenate([x] * reps, axis=1)


def _flash_tile(s, cbm, vt, m_ref, acc_ref, idx, l_ref=None, v_transposed=False):
    m_prev = m_ref[idx]
    m_new = jnp.maximum(m_prev, jnp.max(s, axis=1, keepdims=True) + cbm)
    alpha = jnp.exp2(m_prev - m_new)
    p = jnp.exp2(s - _lane_tile(m_new - cbm, s.shape[1] // 128))
    if l_ref is not None:
        l_ref[idx] = alpha * l_ref[idx] + jnp.sum(p, axis=1, keepdims=True)
    reps = acc_ref.shape[-1] // 128
    a = _lane_tile(alpha, reps)
    pv = _dot_nt(p.astype(bf16), vt) if v_transposed else _dot(p.astype(bf16), vt)
    acc_ref[idx] = a * acc_ref[idx] + pv
    m_ref[idx] = m_new


def _init_state(m_ref, acc_ref, l_ref=None):
    m_ref[...] = jnp.full(m_ref.shape, M_INIT, f32)
    acc_ref[...] = jnp.zeros(acc_ref.shape, f32)
    if l_ref is not None:
        l_ref[...] = jnp.zeros(l_ref.shape, f32)


def _norm_low(acc):
    l = pltpu.roll(acc, 64, 1)
    return jnp.where(_lane(acc.shape[0]) < 64, acc / jnp.maximum(l, TINY), 0.0)


def _cmp_body(tq, pos0, n_pick, qn_ref, gate_ref, slab_ref, bd_ref, m5_ref, oc_ref, mneg_ref):
    i = pl.program_id(1)
    n_cmp = slab_ref.shape[1]
    t = pos0 + i * tq + lax.broadcasted_iota(jnp.int32, (tq, n_cmp), 0)
    n = lax.broadcasted_iota(jnp.int32, (tq, n_cmp), 1)
    dist = t - (CMP_STRIDE * n + CMP_STRIDE - 1)
    valid = (dist >= 0) & (n >= 1)
    idx = jnp.clip(dist, 0, 127)
    low = _lane(tq) < 64
    blk = lax.broadcasted_iota(jnp.int32, (tq, N_BLK_LANES), 1)
    blkf = blk.astype(f32)
    tb = pos0 + i * tq + lax.broadcasted_iota(jnp.int32, (tq, N_BLK_LANES), 0)
    forced = (blk == tb // SEL_BLOCK) | (blk == 0)
    allowed = blk * SEL_BLOCK <= tb
    small = tq % 16 != 0
    kc = slab_ref[0, :, 0:128]
    kc = kc.astype(f32) if small else kc
    scores = []
    for hk in range(NSA_KV_HEADS):
        vc = slab_ref[0, :, 128 + hk * 128:256 + hk * 128]
        imp = jnp.zeros((tq, n_cmp), f32)
        for g in range(NSA_GROUP):
            h = hk * NSA_GROUP + g
            q = qn_ref[0, :, h * 128:(h + 1) * 128]
            s = _dot_nt(q.astype(f32) if small else q, kc)
            tab = jnp.broadcast_to(bd_ref[h:h + 1, :], (tq, 128))
            bias = jnp.concatenate(
                [jnp.take_along_axis(tab, idx[:, c * 128:(c + 1) * 128], axis=1) for c in range(n_cmp // 128)],
                axis=1)
            s = jnp.where(valid, s + bias, NEG_INF)
            p = jnp.exp2(s - jnp.max(s, axis=1, keepdims=True)) * valid.astype(f32)
            p = p / jnp.maximum(jnp.sum(p, axis=1, keepdims=True), TINY)
            imp = imp + p
            o = _dot(p, vc.astype(f32)) if small else _dot(p.astype(bf16), vc)
            gate = gate_ref[0, :, h * 3:h * 3 + 1]
            oc_ref[0, :, h * 128:(h + 1) * 128] = jnp.where(low, o * gate, 0.0)
        imp_hi = imp.astype(bf16)
        imp_lo = (imp - imp_hi.astype(f32)).astype(bf16)
        if small:
            m5 = m5_ref[...].astype(f32)
            p_slc = _dot(imp_hi.astype(f32), m5) + _dot(imp_lo.astype(f32), m5)
        else:
            p_slc = _dot(imp_hi, m5_ref[...]) + _dot(imp_lo, m5_ref[...])
        scores.append(jnp.where(forced, FORCE_SCORE, jnp.where(allowed, p_slc, -FORCE_SCORE)))
    score = jnp.concatenate(scores, axis=0)
    blk2 = jnp.concatenate([blkf] * NSA_KV_HEADS, axis=0)
    chosen = jnp.zeros(score.shape, jnp.bool_)
    for _ in range(n_pick):
        mx = jnp.max(score, axis=1, keepdims=True)
        first = jnp.min(jnp.where(score == mx, blk2, 4096.0), axis=1, keepdims=True)
        pick = blk2 == first
        chosen = chosen | pick
        score = jnp.where(pick, -3e38, score)
    mneg = jnp.where(chosen, 0.0, NEG_INF)
    for hk in range(NSA_KV_HEADS):
        mneg_ref[0, :, hk * 256:(hk + 1) * 256] = mneg[hk * tq:(hk + 1) * tq].astype(bf16)


def _cmp_attn(qn, gates, slab, bd, m5, tq, pos0):
    b, t, _ = qn.shape
    n_cmp = slab.shape[1]
    n_pick = SEL_TOPK if (pos0 + t - 1) // SEL_BLOCK < N_BLK_LANES else SEL_TOPK - 1
    return pl.pallas_call(
        functools.partial(_cmp_body, tq, pos0, n_pick),
        grid=(b, t // tq),
        in_specs=[pl.BlockSpec((1, tq, 768), lambda bb, i: (bb, i, 0)),
                  pl.BlockSpec((1, tq, 128), lambda bb, i: (bb, i, 0)),
                  pl.BlockSpec((1, n_cmp, CMP_OUT), lambda bb, i: (bb, 0, 0)),
                  pl.BlockSpec((16, 128), lambda bb, i: (0, 0)),
                  pl.BlockSpec((n_cmp, N_BLK_LANES), lambda bb, i: (0, 0))],
        out_specs=[pl.BlockSpec((1, tq, 768), lambda bb, i: (bb, i, 0)),
                   pl.BlockSpec((1, tq, 512), lambda bb, i: (bb, i, 0))],
        out_shape=[jax.ShapeDtypeStruct((b, t, 768), f32), jax.ShapeDtypeStruct((b, t, 512), bf16)],
        compiler_params=_params(("parallel", "parallel")),
        name="cmp_attn",
    )(qn, gates, slab, bd, m5)


def _stack_q(qn_ref, hk):
    return jnp.concatenate([qn_ref[:, (hk * NSA_GROUP + g) * 128:(hk * NSA_GROUP + g + 1) * 128]
                            for g in range(NSA_GROUP)], axis=0)


def _stack_rows(ref, hk):
    return jnp.concatenate([ref[hk * NSA_GROUP + g] for g in range(NSA_GROUP)], axis=0)


def _far_rows(far_ref, hk):
    return jnp.concatenate([jnp.broadcast_to(far_ref[0:1, hk * NSA_GROUP + g:hk * NSA_GROUP + g + 1], (TQ, 128))
                            for g in range(NSA_GROUP)], axis=0)


def _sel_body(qn_ref, mneg_ref, gate_ref, kaug_ref, v_ref, d0_ref, d1_ref, far_ref, os_ref,
              lhs_ref, cb_ref, m_ref, acc_ref):
    i = pl.program_id(0)
    _init_state(m_ref, acc_ref)
    for hk in range(NSA_KV_HEADS):
        q = _stack_q(qn_ref, hk)
        mn = mneg_ref[:, hk * 256:(hk + 1) * 256]
        lhs_ref[hk, 0] = jnp.concatenate([jnp.concatenate([mn[:, :128]] * NSA_GROUP, axis=0), q], axis=1)
        lhs_ref[hk, 1] = jnp.concatenate([jnp.concatenate([mn[:, 128:]] * NSA_GROUP, axis=0), q], axis=1)
        cb_ref[hk] = _far_rows(far_ref, hk)

    def tile(hk, j, half, bias_ref):
        k0 = pl.multiple_of(j * TQ, TQ)
        s = _dot_nt(lhs_ref[hk, half], kaug_ref[pl.ds(k0, TQ), :])
        vt = v_ref[pl.ds(k0, TQ), hk * 128:(hk + 1) * 128]
        if bias_ref is None:
            _flash_tile(s, cb_ref[hk], vt, m_ref, acc_ref, hk)
        else:
            _flash_tile(s + _stack_rows(bias_ref, hk), 0.0, vt, m_ref, acc_ref, hk)

    n_far = jnp.maximum(i - 1, 0)
    split = 128 * SEL_BLOCK // TQ

    def far(half):
        def body(j, c):
            for hk in range(NSA_KV_HEADS):
                tile(hk, j, half, None)
            return c
        return body

    lax.fori_loop(0, jnp.minimum(n_far, split), far(0), 0)
    lax.fori_loop(split, jnp.maximum(n_far, split), far(1), 0)

    def near(j, bias_ref):
        for half in range(2):
            @pl.when((j >= split) == (half == 1))
            def _():
                for hk in range(NSA_KV_HEADS):
                    tile(hk, j, half, bias_ref)

    @pl.when(i >= 1)
    def _():
        near(i - 1, d1_ref)

    near(i, d0_ref)
    for hk in range(NSA_KV_HEADS):
        o = _norm_low(acc_ref[hk])
        for g in range(NSA_GROUP):
            h = hk * NSA_GROUP + g
            os_ref[:, h * 128:(h + 1) * 128] = o[g * TQ:(g + 1) * TQ] * gate_ref[:, h * 3 + 1:h * 3 + 2]


def _sel_attn(qn, mneg, gates, kaug, v, d0, d1, far):
    t = qn.shape[0]
    rows = NSA_GROUP * TQ
    tile_spec = pl.BlockSpec((NSA_HEADS, TQ, TQ), lambda i: (0, 0, 0))
    return pl.pallas_call(
        _sel_body,
        grid=(t // TQ,),
        in_specs=[pl.BlockSpec((TQ, 768), lambda i: (i, 0)),
                  pl.BlockSpec((TQ, 512), lambda i: (i, 0)),
                  pl.BlockSpec((TQ, 128), lambda i: (i, 0)),
                  _resident((t, 256)), _resident((t, 256)),
                  tile_spec, tile_spec,
                  pl.BlockSpec((8, 128), lambda i: (0, 0))],
        out_specs=pl.BlockSpec((TQ, 768), lambda i: (i, 0)),
        out_shape=jax.ShapeDtypeStruct((t, 768), f32),
        scratch_shapes=[pltpu.VMEM((NSA_KV_HEADS, 2, rows, 256), bf16),
                        pltpu.VMEM((NSA_KV_HEADS, rows, 128), f32),
                        pltpu.VMEM((NSA_KV_HEADS, rows, 128), f32),
                        pltpu.VMEM((NSA_KV_HEADS, rows, 128), f32)],
        compiler_params=_params(("parallel",)),
        name="sel_attn",
    )(qn, mneg, gates, kaug, v, d0, d1, far)


def _win_body(qn_ref, gate_ref, k_ref, v_ref, d0_ref, d1_ref, d2_ref, ow_ref, m_ref, acc_ref):
    i = pl.program_id(0)
    _init_state(m_ref, acc_ref)

    def tiles(j, bias_ref):
        k0 = pl.multiple_of(j * TQ, TQ)
        kt = k_ref[pl.ds(k0, TQ), :]
        for hk in range(NSA_KV_HEADS):
            s = _dot_nt(_stack_q(qn_ref, hk), kt) + _stack_rows(bias_ref, hk)
            _flash_tile(s, 0.0, v_ref[pl.ds(k0, TQ), hk * 128:(hk + 1) * 128], m_ref, acc_ref, hk)

    @pl.when(i >= 2)
    def _():
        tiles(i - 2, d2_ref)

    @pl.when(i >= 1)
    def _():
        tiles(i - 1, d1_ref)

    tiles(i, d0_ref)
    for hk in range(NSA_KV_HEADS):
        o = _norm_low(acc_ref[hk])
        for g in range(NSA_GROUP):
            h = hk * NSA_GROUP + g
            ow_ref[:, h * 128:(h + 1) * 128] = o[g * TQ:(g + 1) * TQ] * gate_ref[:, h * 3 + 2:h * 3 + 3]


def _win_attn(qn, gates, k, v, d0, d1, d2):
    t = qn.shape[0]
    rows = NSA_GROUP * TQ
    tile_spec = pl.BlockSpec((NSA_HEADS, TQ, TQ), lambda i: (0, 0, 0))
    return pl.pallas_call(
        _win_body,
        grid=(t // TQ,),
        in_specs=[pl.BlockSpec((TQ, 768), lambda i: (i, 0)),
                  pl.BlockSpec((TQ, 128), lambda i: (i, 0)),
                  _resident((t, 128)), _resident((t, 256)),
                  tile_spec, tile_spec, tile_spec],
        out_specs=pl.BlockSpec((TQ, 768), lambda i: (i, 0)),
        out_shape=jax.ShapeDtypeStruct((t, 768), f32),
        scratch_shapes=[pltpu.VMEM((NSA_KV_HEADS, rows, 128), f32), pltpu.VMEM((NSA_KV_HEADS, rows, 128), f32)],
        compiler_params=_params(("parallel",)),
        name="win_attn",
    )(qn, gates, k, v, d0, d1, d2)


def _diff_lambda(dl, lam_init):
    return (jnp.exp(jnp.sum(dl[0:1] * dl[1:2], axis=1, keepdims=True))
            - jnp.exp(jnp.sum(dl[2:3] * dl[3:4], axis=1, keepdims=True)) + lam_init)


def _subln(od, lam_init, sg):
    y = od * lax.rsqrt(jnp.sum(od * od, axis=1, keepdims=True) * (1.0 / DIFF_V) + EPS) * sg
    return y * (1.0 - lam_init)


def _diff_body(lam_init, qd_ref, k_ref, v_ref, d0_ref, d1_ref, far_ref, dl_ref, sg_ref, od_ref, q_ref, m_ref, acc_ref):
    i = pl.program_id(0)
    lane = _lane(TQ)
    lam = _diff_lambda(dl_ref[...], lam_init)
    n_far = jnp.maximum(i - 1, 0)
    j1 = jnp.maximum(i - 1, 0)
    for h in range(DIFF_HEADS):
        qc = qd_ref[:, h * 128:(h + 1) * 128]
        o = (h % 2) * 64
        zero = jnp.zeros_like(qc)
        q_ref[h] = jnp.concatenate([jnp.where((lane >= o) & (lane < o + DIFF_QK), qc, zero),
                                    jnp.where((lane >= o + DIFF_QK) & (lane < o + 64), qc, zero)], axis=0)
    m_ref[...] = jnp.full(m_ref.shape, M_INIT, f32)
    acc_ref[...] = jnp.zeros(acc_ref.shape, f32)

    def logits(h, j):
        k0 = pl.multiple_of(j * TQ, TQ)
        return _dot_nt(q_ref[h], k_ref[pl.ds(k0, TQ), (h // 2) * 128:(h // 2 + 1) * 128])

    def near_logits(h):
        b1, b0 = d1_ref[h], d0_ref[h]
        s1 = jnp.where(i >= 1, logits(h, j1) + jnp.concatenate([b1, b1], axis=0), NEG_INF)
        s0 = logits(h, i) + jnp.concatenate([b0, b0], axis=0)
        return s1, s0

    def far_max(j, c):
        for h in range(DIFF_HEADS):
            s = logits(h, j)
            m_ref[h] = jnp.maximum(m_ref[h], jnp.maximum(s[:, :128], s[:, 128:]))
        return c

    lax.fori_loop(0, n_far, far_max, 0)
    for h in range(DIFF_HEADS):
        s1, s0 = near_logits(h)
        m = jnp.maximum(jnp.max(m_ref[h], axis=1, keepdims=True) + far_ref[0:1, h:h + 1],
                        jnp.maximum(jnp.max(s1, axis=1, keepdims=True), jnp.max(s0, axis=1, keepdims=True)))
        m_ref[h] = jnp.broadcast_to(m, (2 * TQ, 128))

    def accumulate(h, s, j):
        k0 = pl.multiple_of(j * TQ, TQ)
        p = jnp.exp2(s)
        acc_ref[h] += _dot(p.astype(bf16), v_ref[pl.ds(k0, TQ), h * 128:(h + 1) * 128])

    def far_acc(j, c):
        for h in range(DIFF_HEADS):
            shift = m_ref[h] - far_ref[0:1, h:h + 1]
            accumulate(h, logits(h, j) - _lane_tile(shift, TQ // 128), j)
        return c

    lax.fori_loop(0, n_far, far_acc, 0)
    for h in range(DIFF_HEADS):
        s1, s0 = near_logits(h)
        m = _lane_tile(m_ref[h], TQ // 128)
        accumulate(h, s1 - m, j1)
        accumulate(h, s0 - m, i)
    for h in range(DIFF_HEADS):
        o = _norm_low(acc_ref[h])
        od_ref[:, h * 128:(h + 1) * 128] = _subln(o[:TQ] - lam * o[TQ:], lam_init, sg_ref[...])


def _diff_attn(qd, k, v, d0, d1, far, dl, sg, lam_init):
    t = qd.shape[0]
    tile_spec = pl.BlockSpec((DIFF_HEADS, TQ, TQ), lambda i: (0, 0, 0))
    return pl.pallas_call(
        functools.partial(_diff_body, lam_init),
        grid=(t // TQ,),
        in_specs=[pl.BlockSpec((TQ, 512), lambda i: (i, 0)),
                  _resident((t, 256)), _resident((t, 512)),
                  tile_spec, tile_spec,
                  pl.BlockSpec((8, 128), lambda i: (0, 0)),
                  pl.BlockSpec((4, DIFF_QK), lambda i: (0, 0)),
                  pl.BlockSpec((1, 128), lambda i: (0, 0))],
        out_specs=pl.BlockSpec((TQ, 512), lambda i: (i, 0)),
        out_shape=jax.ShapeDtypeStruct((t, 512), f32),
        scratch_shapes=[pltpu.VMEM((DIFF_HEADS, 2 * TQ, 128), bf16),
                        pltpu.VMEM((DIFF_HEADS, 2 * TQ, 128), f32),
                        pltpu.VMEM((DIFF_HEADS, 2 * TQ, 128), f32)],
        compiler_params=_params(("parallel",)),
        name="diff_attn",
    )(qd, k, v, d0, d1, far, dl, sg)


def _rows48(ref):
    x = ref[0].astype(f32)
    return jnp.concatenate([x[:, h * 128:(h + 1) * 128] for h in range(NSA_HEADS)], axis=0)


def _pad_keys(new):
    return jnp.concatenate([new, jnp.zeros((PAGE - new.shape[0], new.shape[1]), f32)], axis=0)


def _sel_dec_body(*refs):
    pages = refs[1:1 + DEC_PAGES]
    (qn_ref, mneg_ref, gate_ref, new_ref, z_ref, blast_ref, bnew_ref, far_ref, os_ref,
     m_ref, l_ref, acc_ref) = refs[1 + DEC_PAGES:]
    g = pl.program_id(1)
    last = pl.num_programs(1) - 1

    @pl.when(g == 0)
    def _():
        _init_state(m_ref, acc_ref, l_ref)

    q = _rows48(qn_ref).astype(bf16)
    mn = mneg_ref[0].astype(f32)
    mrows = jnp.concatenate([mn[:, (h // NSA_GROUP) * 256:(h // NSA_GROUP + 1) * 256] for h in range(NSA_HEADS)],
                            axis=0).astype(bf16)
    k_t = jnp.concatenate([p[0, 0] for p in pages], axis=1).astype(bf16)
    v_t = jnp.concatenate([p[0, 1] for p in pages], axis=1).astype(bf16)
    blocks_per_step = DEC_KEYS // SEL_BLOCK
    z = z_ref[pl.ds(pl.multiple_of(N_BLK_LANES - g * blocks_per_step, blocks_per_step), N_BLK_LANES), :]
    cb = jnp.concatenate([jnp.broadcast_to(far_ref[0:1, h:h + 1], (8, DEC_KEYS)) for h in range(NSA_HEADS)], axis=0)
    s = _dot(q, k_t) + _dot(mrows, z) + jnp.where(g == last, blast_ref[...], cb)
    _flash_tile(s, 0.0, v_t, m_ref, acc_ref, 0, l_ref, v_transposed=True)

    @pl.when(g == last)
    def _():
        kn = _pad_keys(new_ref[0]).astype(bf16)
        _flash_tile(_dot_nt(q, kn[:, 0:128]) + bnew_ref[...], 0.0, kn[:, 128:256], m_ref, acc_ref, 0, l_ref)
        o = acc_ref[0] / jnp.maximum(l_ref[0], TINY)
        low = _lane(8) < 64
        for h in range(NSA_HEADS):
            oh = o[h * 8:(h + 1) * 8]
            if h // NSA_GROUP == 1:
                oh = pltpu.roll(oh, 64, 1)
            os_ref[0, :, h * 128:(h + 1) * 128] = jnp.where(low, oh * gate_ref[0, :, h * 3 + 1:h * 3 + 2], 0.0)


def _paged_specs(page_shape):
    def page_spec(i):
        return pl.BlockSpec((1,) + page_shape, lambda bb, g, tab: (tab[bb, g * DEC_PAGES + i], 0, 0, 0))
    return [page_spec(i) for i in range(DEC_PAGES)]


def _sel_dec(pool, table, qn, mneg, gates, new, zbig, blast, bnew, far):
    b, n_pages = table.shape
    c3 = lambda wd: pl.BlockSpec((1, 8, wd), lambda bb, g, tab: (bb, 0, 0))
    full = lambda a: pl.BlockSpec(a.shape, lambda bb, g, tab: (0,) * a.ndim)
    gs = pltpu.PrefetchScalarGridSpec(
        num_scalar_prefetch=1,
        grid=(b, n_pages // DEC_PAGES),
        in_specs=_paged_specs((2, 128, PAGE)) + [c3(768), c3(512), c3(128), c3(256), full(zbig), full(blast),
                                                 full(bnew), full(far)],
        out_specs=c3(768),
        scratch_shapes=[pltpu.VMEM((1, 48, 128), f32), pltpu.VMEM((1, 48, 128), f32), pltpu.VMEM((1, 48, 128), f32)],
    )
    return pl.pallas_call(
        _sel_dec_body,
        grid_spec=gs,
        out_shape=jax.ShapeDtypeStruct((b, 8, 768), f32),
        compiler_params=_params(("parallel", "arbitrary")),
        name="sel_dec",
    )(table, *([pool] * DEC_PAGES), qn, mneg, gates, new, zbig, blast, bnew, far)


def _win_dec_body(qn_ref, gate_ref, past_ref, new_ref, bias_ref, ow_ref, keep_ref):
    q = _rows48(qn_ref).astype(bf16)
    past = past_ref[0]
    n_past = past.shape[0]
    kv = jnp.concatenate([past, _pad_keys(new_ref[0])], axis=0).astype(bf16)
    s = _dot_nt(q, kv[:, 0:128]) + bias_ref[...]
    p = jnp.exp2(s - jnp.max(s, axis=1, keepdims=True))
    o = _dot(p.astype(bf16), kv[:, 128:256]) / jnp.maximum(jnp.sum(p, axis=1, keepdims=True), TINY)
    low = _lane(8) < 64
    for h in range(NSA_HEADS):
        oh = o[h * 8:(h + 1) * 8]
        if h // NSA_GROUP == 1:
            oh = pltpu.roll(oh, 64, 1)
        ow_ref[0, :, h * 128:(h + 1) * 128] = jnp.where(low, oh * gate_ref[0, :, h * 3 + 2:h * 3 + 3], 0.0)
    keep_ref[0, 0:n_past - 8, :] = past[8:, :]
    keep_ref[0, n_past - 8:n_past, :] = new_ref[0]


def _win_dec(qn, gates, past, new, bias):
    b, n_past, _ = past.shape
    c3 = lambda wd: pl.BlockSpec((1, 8, wd), lambda bb: (bb, 0, 0))
    return pl.pallas_call(
        _win_dec_body,
        grid=(b,),
        in_specs=[c3(768), c3(128), pl.BlockSpec((1, n_past, 256), lambda bb: (bb, 0, 0)), c3(256),
                  pl.BlockSpec(bias.shape, lambda bb: (0, 0))],
        out_specs=[c3(768), pl.BlockSpec((1, n_past, 256), lambda bb: (bb, 0, 0))],
        out_shape=[jax.ShapeDtypeStruct((b, 8, 768), f32), jax.ShapeDtypeStruct((b, n_past, 256), f32)],
        compiler_params=_params(("parallel",)),
        name="win_dec",
    )(qn, gates, past, new, bias)


def _diff_dec_body(lam_init, *refs):
    pages = refs[1:1 + DEC_PAGES]
    (qd_ref, new_ref, blast_ref, bnew_ref, far_ref, dl_ref, sg_ref, od_ref, m_ref, l_ref, acc_ref) = refs[1 + DEC_PAGES:]
    g = pl.program_id(1)
    last = pl.num_programs(1) - 1

    @pl.when(g == 0)
    def _():
        _init_state(m_ref, acc_ref, l_ref)

    lane = _lane(8)
    x = qd_ref[0].astype(f32)
    qs = []
    for h in range(DIFF_HEADS):
        qc = x[:, h * 128:(h + 1) * 128]
        if h % 2 == 1:
            qc = pltpu.roll(qc, 64, 1)
        qs.append(jnp.concatenate([jnp.where(lane < DIFF_QK, qc, 0.0),
                                   jnp.where((lane >= DIFF_QK) & (lane < 2 * DIFF_QK), qc, 0.0)], axis=0).astype(bf16))
    for h in range(DIFF_HEADS):
        kv = jnp.concatenate([p[0, :, h, :] for p in pages], axis=0).astype(bf16)
        bias = jnp.where(g == last, blast_ref[h * 16:(h + 1) * 16, :], far_ref[0:1, h:h + 1])
        _flash_tile(_dot_nt(qs[h], kv) + bias, 0.0, kv, m_ref, acc_ref, h, l_ref)

    @pl.when(g == last)
    def _():
        lam = _diff_lambda(dl_ref[...], lam_init)
        for h in range(DIFF_HEADS):
            kn = _pad_keys(new_ref[0, :, h * 128:(h + 1) * 128]).astype(bf16)
            _flash_tile(_dot_nt(qs[h], kn) + bnew_ref[h * 16:(h + 1) * 16, :], 0.0, kn, m_ref, acc_ref, h, l_ref)
            a = acc_ref[h] / jnp.maximum(l_ref[h], TINY)
            od = pltpu.roll(a[0:8] - lam * a[8:16], 64, 1)
            od_ref[0, :, h * 128:(h + 1) * 128] = _subln(jnp.where(lane < 64, od, 0.0), lam_init, sg_ref[...])


def _diff_dec(pool, table, qd, new, blast, bnew, far, dl, sg, lam_init):
    b, n_pages = table.shape
    c3 = lambda wd: pl.BlockSpec((1, 8, wd), lambda bb, g, tab: (bb, 0, 0))
    full = lambda a: pl.BlockSpec(a.shape, lambda bb, g, tab: (0,) * a.ndim)
    gs = pltpu.PrefetchScalarGridSpec(
        num_scalar_prefetch=1,
        grid=(b, n_pages // DEC_PAGES),
        in_specs=_paged_specs((PAGE, DIFF_HEADS, DIFF_ROW)) + [c3(512), c3(512), full(blast), full(bnew), full(far),
                                                               full(dl), full(sg)],
        out_specs=c3(512),
        scratch_shapes=[pltpu.VMEM((DIFF_HEADS, 16, 128), f32), pltpu.VMEM((DIFF_HEADS, 16, 128), f32),
                        pltpu.VMEM((DIFF_HEADS, 16, 128), f32)],
    )
    return pl.pallas_call(
        functools.partial(_diff_dec_body, lam_init),
        grid_spec=gs,
        out_shape=jax.ShapeDtypeStruct((b, 8, 512), f32),
        compiler_params=_params(("parallel", "arbitrary")),
        name="diff_dec",
    )(table, *([pool] * DEC_PAGES), qd, new, blast, bnew, far, dl, sg)


LRU_TT = 256


def _lru_body(tt, x_ref, gi_ref, cbuf_ref, h0_ref, cw_ref, cb_ref, wa_ref, ba_ref, wx_ref, bx_ref, lam_ref,
              y_ref, hl_ref, cn_ref, xe_ref, a_ref, b_ref, h_ref):
    j = pl.program_id(1)

    @pl.when(j == 0)
    def _():
        xe_ref[0:8, :] = jnp.zeros((8, LRU_WIDTH), f32)
        xe_ref[5:8, :] = cbuf_ref[0]
        h_ref[0:1, :] = h0_ref[0]

    x = x_ref[0]
    xe_ref[8:8 + tt, :] = x
    xc = cb_ref[...] + sum(xe_ref[5 + k:5 + k + tt, :] * cw_ref[k:k + 1, :] for k in range(CONV_W))
    xcb = xc.astype(bf16)
    r = _sigmoid(_dot(xcb, wa_ref[...]) + ba_ref[...])
    ig = _sigmoid(_dot(xcb, wx_ref[...]) + bx_ref[...])
    lam = lam_ref[...]
    softplus = jnp.maximum(-lam, 0.0) + jnp.log(1.0 + jnp.exp(-jnp.abs(lam)))
    log_a = -LRU_C * r * softplus
    a = jnp.exp(log_a)
    a_ref[...] = a
    b_ref[...] = jnp.sqrt(1.0 - jnp.exp(2.0 * log_a)) * (ig * xc)

    def step(k, h):
        h = a_ref[pl.ds(k, 1), :] * h + b_ref[pl.ds(k, 1), :]
        b_ref[pl.ds(k, 1), :] = h
        return h

    h = lax.fori_loop(0, tt, step, h_ref[0:1, :])
    h_ref[0:1, :] = h
    y_ref[0] = b_ref[...] * _gelu(gi_ref[0])
    hl_ref[0] = h
    cn_ref[0] = xe_ref[5 + tt:8 + tt, :]
    xe_ref[5:8, :] = xe_ref[5 + tt:8 + tt, :]


def _rglru(x, gate_in, conv_buf, h0, cw, cb, wa, ba, wx, bx, lam):
    b, t, w = x.shape
    tt = min(LRU_TT, t)

    def bd(wblk):
        m = jnp.zeros((w, w), f32)
        for n in range(LRU_BLOCKS):
            m = m.at[n * LRU_BLOCK:(n + 1) * LRU_BLOCK, n * LRU_BLOCK:(n + 1) * LRU_BLOCK].set(wblk[n])
        return m.astype(bf16)

    vec = lambda: pl.BlockSpec((1, w), lambda bb, j: (0, 0))
    return pl.pallas_call(
        functools.partial(_lru_body, tt),
        grid=(b, t // tt),
        in_specs=[pl.BlockSpec((1, tt, w), lambda bb, j: (bb, j, 0)),
                  pl.BlockSpec((1, tt, w), lambda bb, j: (bb, j, 0)),
                  pl.BlockSpec((1, 3, w), lambda bb, j: (bb, 0, 0)),
                  pl.BlockSpec((1, 1, w), lambda bb, j: (bb, 0, 0)),
                  pl.BlockSpec((CONV_W, w), lambda bb, j: (0, 0)), vec(),
                  pl.BlockSpec((w, w), lambda bb, j: (0, 0)), vec(),
                  pl.BlockSpec((w, w), lambda bb, j: (0, 0)), vec(), vec()],
        out_specs=[pl.BlockSpec((1, tt, w), lambda bb, j: (bb, j, 0)),
                   pl.BlockSpec((1, 1, w), lambda bb, j: (bb, 0, 0)),
                   pl.BlockSpec((1, 3, w), lambda bb, j: (bb, 0, 0))],
        out_shape=[jax.ShapeDtypeStruct((b, t, w), f32), jax.ShapeDtypeStruct((b, 1, w), f32),
                   jax.ShapeDtypeStruct((b, 3, w), f32)],
        scratch_shapes=[pltpu.VMEM((tt + 8, w), f32), pltpu.VMEM((tt, w), f32), pltpu.VMEM((tt, w), f32),
                        pltpu.VMEM((8, w), f32)],
        compiler_params=_params(("parallel", "arbitrary")),
        name="rglru",
    )(x, gate_in, conv_buf, h0.reshape(b, 1, w), cw, cb.reshape(1, w), bd(wa), ba.reshape(1, w), bd(wx),
      bx.reshape(1, w), lam.reshape(1, w))


def _outproj_body(x_ref, gt_ref, oc_ref, os_ref, ow_ref, od_ref, yr_ref, w_ref, o_ref):
    on = oc_ref[...] + os_ref[...] + ow_ref[...]
    cat = jnp.concatenate([on.astype(bf16), od_ref[...].astype(bf16), yr_ref[...].astype(bf16)], axis=1)
    o_ref[...] = x_ref[...] + gt_ref[...] * _dot(cat, w_ref[...])


def _outproj(x, gt, oc, os_, ow, od, yr, w, tm):
    m = x.shape[0]
    row = lambda wd: pl.BlockSpec((tm, wd), lambda i: (i, 0))
    return pl.pallas_call(
        _outproj_body,
        grid=(m // tm,),
        in_specs=[row(D_MODEL), _rowspec(gt.shape[0], tm, D_MODEL), row(768), row(768), row(768), row(512),
                  row(384), pl.BlockSpec((N_OUT_IN, D_MODEL), lambda i: (0, 0))],
        out_specs=row(D_MODEL),
        out_shape=jax.ShapeDtypeStruct((m, D_MODEL), f32),
        compiler_params=_params(("parallel",)),
        name="outproj",
    )(x, gt, oc, os_, ow, od, yr, w)


def _ffn_body(n_exp, final, x_ref, g_ref, sc_ref, sh_ref, gt_ref, rw_ref, rb_ref, w1_ref, w3_ref, w2_ref, fg_ref,
              o_ref, h_ref, acc_ref, gate_ref):
    e, f = pl.program_id(1), pl.program_id(2)

    @pl.when((e == 0) & (f == 0))
    def _():
        h = _normmod(x_ref[...], g_ref[...], sc_ref[...], sh_ref[...])
        h_ref[...] = h.astype(bf16)
        acc_ref[...] = jnp.zeros_like(acc_ref)
        if n_exp > 1:
            logits = jnp.dot(h, rw_ref[...], preferred_element_type=f32, precision=lax.Precision.HIGHEST)
            lane = lax.broadcasted_iota(jnp.int32, logits.shape, 1)
            logits = jnp.where(lane < n_exp, logits + rb_ref[...], NEG_INF)
            v1 = jnp.max(logits, axis=1, keepdims=True)
            i1 = jnp.min(jnp.where(logits == v1, lane, 4096), axis=1, keepdims=True)
            rest = jnp.where(lane == i1, NEG_INF, logits)
            v2 = jnp.max(rest, axis=1, keepdims=True)
            i2 = jnp.min(jnp.where(rest == v2, lane, 4096), axis=1, keepdims=True)
            e2 = jnp.exp(v2 - v1)
            w_1 = 1.0 / (1.0 + e2)
            w_2 = e2 / (1.0 + e2)
            gate = jnp.where(lane == i1, w_1, 0.0) + jnp.where(lane == i2, w_2, 0.0)
            for k in range(n_exp):
                gate_ref[k] = gate[:, k:k + 1]

    hb = h_ref[...]
    hid = _silu(_dot(hb, w1_ref[0])) * _dot(hb, w3_ref[0])
    if n_exp > 1:
        hid = hid * gate_ref[e]
    acc_ref[...] += _dot(hid.astype(bf16), w2_ref[0])

    @pl.when((e == n_exp - 1) & (f == pl.num_programs(2) - 1))
    def _():
        y = x_ref[...] + gt_ref[...] * acc_ref[...]
        if final:
            y = y * lax.rsqrt(jnp.mean(y * y, axis=-1, keepdims=True) + EPS) * fg_ref[...]
        o_ref[...] = y


def _ffn(x, g, sc, sh, gt, rw, rb, w1, w3, w2, fg, final, tm, tf):
    m = x.shape[0]
    n_exp, _, ff = w1.shape
    vec = lambda: pl.BlockSpec((1, D_MODEL), lambda i, e, f: (0, 0))

    def rowspec(a):
        if a.shape[0] == 1:
            return vec()
        return pl.BlockSpec((tm, D_MODEL), lambda i, e, f: (i, 0))

    return pl.pallas_call(
        functools.partial(_ffn_body, n_exp, final),
        grid=(m // tm, n_exp, ff // tf),
        in_specs=[pl.BlockSpec((tm, D_MODEL), lambda i, e, f: (i, 0)), vec(), rowspec(sc), rowspec(sh), rowspec(gt),
                  pl.BlockSpec((D_MODEL, 128), lambda i, e, f: (0, 0)),
                  pl.BlockSpec((1, 128), lambda i, e, f: (0, 0)),
                  pl.BlockSpec((1, D_MODEL, tf), lambda i, e, f: (e, 0, f)),
                  pl.BlockSpec((1, D_MODEL, tf), lambda i, e, f: (e, 0, f)),
                  pl.BlockSpec((1, tf, D_MODEL), lambda i, e, f: (e, f, 0)),
                  vec()],
        out_specs=pl.BlockSpec((tm, D_MODEL), lambda i, e, f: (i, 0)),
        out_shape=jax.ShapeDtypeStruct((m, D_MODEL), f32),
        scratch_shapes=[pltpu.VMEM((tm, D_MODEL), bf16), pltpu.VMEM((tm, D_MODEL), f32),
                        pltpu.VMEM((N_EXPERTS, tm, 1), f32)],
        compiler_params=_params(("parallel", "arbitrary", "arbitrary")),
        name="ffn",
    )(x, g.reshape(1, -1), sc, sh, gt, rw, rb, w1, w3, w2, fg.reshape(1, -1))


def _t5_bucket(dist):
    n = jnp.maximum(dist, 0)
    exact = N_BUCKETS // 2
    nf = jnp.maximum(n, 1).astype(f32)
    large = exact + (jnp.log(nf / exact) / math.log(MAX_DISTANCE / exact) * (N_BUCKETS - exact)).astype(jnp.int32)
    return jnp.where(n < exact, n, jnp.minimum(large, N_BUCKETS - 1))


def _by_dist(bd, dist, valid):
    onehot = jax.nn.one_hot(jnp.clip(dist, 0, 127), 128, dtype=f32)
    v = jnp.einsum('...d,dh->h...', onehot, bd, precision=lax.Precision.HIGHEST)
    return jnp.where(valid[None], v, NEG_INF)


def _bias_tables(rel_bias, pos0_dec):
    nh = rel_bias.shape[1]
    bd = rel_bias[_t5_bucket(jnp.arange(128, dtype=jnp.int32))] * LOG2E
    far = rel_bias[N_BUCKETS - 1] * LOG2E
    r = jnp.arange(TQ, dtype=jnp.int32)[:, None]
    c = jnp.arange(TQ, dtype=jnp.int32)[None, :]
    d0 = _by_dist(bd, r - c, r - c >= 0)
    d1 = _by_dist(bd, r - c + TQ, r - c + TQ >= 0)
    d2 = jnp.where((c > r)[None], far[:, None, None], NEG_INF)
    bd_pad = jnp.zeros((16, 128), f32).at[:nh].set(bd.T)
    far_pad = jnp.zeros((8, 128), f32).at[0, :nh].set(far)
    r8 = jnp.arange(8, dtype=jnp.int32)[:, None]
    ck = jnp.arange(DEC_KEYS, dtype=jnp.int32)[None, :]
    last = _by_dist(bd, DEC_KEYS + r8 - ck, jnp.ones((8, DEC_KEYS), bool))
    cn = jnp.arange(PAGE, dtype=jnp.int32)[None, :]
    new = _by_dist(bd, r8 - cn, (r8 - cn >= 0) & (cn < 8))
    n_win = min(WINDOW, pos0_dec)
    cw = jnp.arange(n_win, dtype=jnp.int32)[None, :]
    dw = n_win + r8 - cw
    win = _by_dist(bd, dw, dw < WINDOW)
    return d0, d1, d2, bd_pad, far_pad, far, last, new, win


def _m5(n_cmp):
    n = np.arange(n_cmp)[:, None]
    j = np.arange(N_BLK_LANES)[None, :]
    return jnp.asarray(((n >= 4 * j) & (n <= 4 * j + 4)).astype(np.float32), bf16)


def _zbig():
    rho = np.arange(2 * N_BLK_LANES)[:, None]
    c = np.arange(DEC_KEYS)[None, :]
    return jnp.asarray((rho == N_BLK_LANES + c // SEL_BLOCK).astype(np.float32), bf16)


def _split6(mod):
    return [mod[:, k * D_MODEL:(k + 1) * D_MODEL] for k in range(6)]


def _prompt_layer(x, mod, lw, tb, l, final_g, last):
    t = x.shape[0]
    sh1, sc1, gt1, sh2, sc2, gt2 = _split6(mod)
    (qn, qd, gates, cmp, sel, win, diff, xr, gr, kaug, vsel, kwin, vwin, kdiff, vdiff) = _proj(
        x, lw['norm_mix_g'], sc1, sh1, lw['w_aug'], lw['onerow'], 256)
    n_pages = t // PAGE
    ident = jnp.arange(n_pages, dtype=jnp.int32).reshape(1, n_pages)
    slab = _compress(cmp.reshape(n_pages, 8, 4096), ident, *lw['cmp'])
    oc, mneg = _cmp_attn(qn[None], gates[None], slab, tb['bd_nsa'], tb['m5_p'], TQ, 0)
    os_ = _sel_attn(qn, mneg[0], gates, kaug, vsel, tb['d0n'], tb['d1n'], tb['far_n'])
    ow = _win_attn(qn, gates, kwin, vwin, tb['d0n'], tb['d1n'], tb['d2n'])
    lam_init = 0.8 - 0.6 * math.exp(-0.3 * l)
    od = _diff_attn(qd, kdiff, vdiff, tb['d0d'], tb['d1d'], tb['far_d'], lw['diff_lambda'], lw['subln'], lam_init)
    zeros_c = jnp.zeros((1, CONV_W - 1, LRU_WIDTH), f32)
    zeros_h = jnp.zeros((1, LRU_WIDTH), f32)
    yr, hl, cn = _rglru(xr[None], gr[None], zeros_c, zeros_h, *lw['lru'])
    x = _outproj(x, gt1, oc[0], os_, ow, od, yr[0], lw['w_out'], 512)
    x = _ffn(x, lw['norm_ffn_g'], sc2, sh2, gt2, *lw['ffn'], final_g, last, 512, 1408)
    n_keep = min(WINDOW, t)
    state = (cmp.reshape(1, t, 2, NSA_KV_HEADS, HEAD_DIM), sel.reshape(1, t, 2, NSA_KV_HEADS, HEAD_DIM),
             diff.reshape(1, t, DIFF_HEADS, DIFF_ROW), win[t - n_keep:].reshape(1, n_keep, 2, NSA_KV_HEADS, HEAD_DIM),
             hl.reshape(1, LRU_WIDTH), cn)
    return x, state


def _sample_layer(x, mod, lw, tb, l, final_g, last, caches, page_table, nb, dec):
    pool_cmp, pool_sel, pool_diff, win_past, h0, conv0 = caches
    pos0 = page_table.shape[1] * PAGE
    table_all = page_table + l * pool_cmp.shape[0]
    rep = lambda a: jnp.repeat(a, dec, axis=0)
    sh1, sc1, gt1, sh2, sc2, gt2 = [rep(m) for m in _split6(mod)]
    m = nb * dec
    (qn, qd, gates, cmp, sel, win, diff, xr, gr, _, _, _, _, _, _) = _proj(
        x, lw['norm_mix_g'], sc1, sh1, lw['w_aug'], lw['onerow'], m)
    b3 = lambda a: a.reshape(nb, dec, a.shape[-1])
    slab = _compress(pool_cmp, page_table, *lw['cmp'])
    oc, mneg = _cmp_attn(b3(qn), b3(gates), slab, tb['bd_nsa'], tb['m5_s'], dec, pos0)
    os_ = _sel_dec(pool_sel, table_all, b3(qn), mneg, b3(gates), b3(sel), tb['zbig'], tb['last_n'], tb['new_n'],
                   tb['far_n'])
    ow, win_keep = _win_dec(b3(qn), b3(gates), win_past, b3(win), tb['win_n'])
    lam_init = 0.8 - 0.6 * math.exp(-0.3 * l)
    od = _diff_dec(pool_diff, table_all, b3(qd), b3(diff), tb['last_d'], tb['new_d'], tb['far_d'],
                   lw['diff_lambda'], lw['subln'], lam_init)
    yr, hl, cn = _rglru(b3(xr), b3(gr), conv0, h0, *lw['lru'])
    flat = lambda a: a.reshape(m, a.shape[-1])
    x = _outproj(x, gt1, flat(oc), flat(os_), flat(ow), flat(od), flat(yr), lw['w_out'], m)
    x = _ffn(x, lw['norm_ffn_g'], sc2, sh2, gt2, *lw['ffn'], final_g, last, m, 1408)
    n_keep = win_keep.shape[1]
    state = (cmp.reshape(nb, dec, 2, NSA_KV_HEADS, HEAD_DIM), sel.reshape(nb, dec, 2, NSA_KV_HEADS, HEAD_DIM),
             diff.reshape(nb, dec, DIFF_HEADS, DIFF_ROW), win_keep.reshape(nb, n_keep, 2, NSA_KV_HEADS, HEAD_DIM),
             hl.reshape(nb, LRU_WIDTH), cn)
    return x, state


def kernel(x_prompt, x_sample, c_prompt, c_sample, cache_nsa_cmp, cache_nsa_sel, cache_diff, cache_nsa_win, state_lru_h, state_lru_conv, page_table, rel_bias, norm_mix_g, norm_ffn_g, final_norm_g, w_ada, b_ada, w_in, cmp_pe, cmp_w1, cmp_b1, cmp_w2, diff_lambda, diff_subln_g, lru_conv_w, lru_conv_b, lru_wa, lru_ba, lru_wx, lru_bx, lru_lambda, w_out, ffn_w1, ffn_w3, ffn_w2, router_w, router_b, moe_w1, moe_w3, moe_w2):
    depth = w_in.shape[0]
    t = x_prompt.shape[1]
    nb, dec = x_sample.shape[0], x_sample.shape[1]
    n_pool = cache_nsa_cmp.shape[1]
    pos0_dec = page_table.shape[1] * PAGE
    cols, onerow = _proj_cols()
    rows = _outproj_rows()
    onerow = jnp.asarray(onerow)
    d0, d1, d2, bd, far_pad, far, last, new, win = _bias_tables(rel_bias, pos0_dec)
    nh = NSA_HEADS
    stack_rows = lambda a: a.reshape(-1, a.shape[-1])
    twice = lambda a: jnp.repeat(a, 2, axis=0)
    tb = {'d0n': d0[:nh], 'd1n': d1[:nh], 'd2n': d2[:nh], 'd0d': d0[nh:], 'd1d': d1[nh:], 'bd_nsa': bd,
          'far_n': far_pad, 'far_d': jnp.zeros((8, 128), f32).at[0, :DIFF_HEADS].set(far[nh:]),
          'm5_p': _m5(t // CMP_STRIDE), 'm5_s': _m5(pos0_dec // CMP_STRIDE), 'zbig': _zbig(),
          'last_n': stack_rows(last[:nh]), 'new_n': stack_rows(new[:nh]),
          'win_n': jnp.concatenate([stack_rows(win[:nh]), stack_rows(new[:nh])], axis=1),
          'last_d': stack_rows(twice(last[nh:])), 'new_d': stack_rows(twice(new[nh:]))}

    c_all = jnp.concatenate([c_prompt, c_sample, jnp.zeros((7, D_MODEL), f32)], axis=0)
    layers = []
    for l in range(depth):
        j = l // 2
        lw = {'norm_mix_g': norm_mix_g[l], 'norm_ffn_g': norm_ffn_g[l],
              'w_aug': _take_cols(w_in[l], cols).astype(bf16), 'onerow': onerow,
              'cmp': _compress_weights(cmp_pe[l], cmp_w1[l], cmp_b1[l], cmp_w2[l]),
              'diff_lambda': diff_lambda[l],
              'subln': jnp.zeros((1, 128), f32).at[0, :64].set(diff_subln_g[l]),
              'lru': (lru_conv_w[l], lru_conv_b[l], lru_wa[l], lru_ba[l], lru_wx[l], lru_bx[l], lru_lambda[l]),
              'w_out': _take_rows(w_out[l], rows).astype(bf16)}
        if l % 2 == 0:
            lw['ffn'] = (jnp.zeros((D_MODEL, 128), f32), jnp.zeros((1, 128), f32),
                         ffn_w1[j][None].astype(bf16), ffn_w3[j][None].astype(bf16), ffn_w2[j][None].astype(bf16))
        else:
            rw = jnp.zeros((D_MODEL, 128), f32).at[:, :N_EXPERTS].set(router_w[j])
            rb = jnp.zeros((1, 128), f32).at[0, :N_EXPERTS].set(router_b[j])
            lw['ffn'] = (rw, rb, moe_w1[j].astype(bf16), moe_w3[j].astype(bf16), moe_w2[j].astype(bf16))
        layers.append(lw)

    mods = [_ada(c_all, w_ada[l], b_ada[l]) for l in range(depth)]

    x = x_prompt[0]
    p_states = []
    for l in range(depth):
        x, st = _prompt_layer(x, mods[l][0:1], layers[l], tb, l, final_norm_g, l == depth - 1)
        p_states.append(st)
    y_prompt = x[None]

    xs = x_sample.reshape(nb * dec, D_MODEL)
    s_states = []
    pool_sel = jnp.transpose(cache_nsa_sel, (0, 1, 3, 4, 5, 2)).reshape(depth * n_pool, 2, 128, PAGE)
    pool_diff = cache_diff.reshape(depth * n_pool, PAGE, DIFF_HEADS, DIFF_ROW)
    for l in range(depth):
        pool_cmp = cache_nsa_cmp[l].reshape(n_pool, PAGE, 256).reshape(n_pool, 8, 4096)
        caches = (pool_cmp, pool_sel, pool_diff, cache_nsa_win[l].reshape(nb, -1, 256),
                  state_lru_h[l], state_lru_conv[l])
        xs, st = _sample_layer(xs, mods[l][1:1 + nb], layers[l], tb, l, final_norm_g, l == depth - 1, caches,
                               page_table, nb, dec)
        s_states.append(st)
    y_sample = xs.reshape(nb, dec, D_MODEL)

    stack = lambda sts, k: jnp.stack([s[k] for s in sts], axis=0)
    return (y_prompt, y_sample, *[stack(p_states, k) for k in range(6)], *[stack(s_states, k) for k in range(6)])
```

```python
import functools
import math

import numpy as np
import jax
import jax.numpy as jnp
from jax import lax
from jax.experimental import pallas as pl
from jax.experimental.pallas import tpu as pltpu

f32 = jnp.float32
bf16 = jnp.bfloat16

D_MODEL = 1024
HEAD_DIM = 64
NSA_HEADS = 6
NSA_KV_HEADS = 2
NSA_GROUP = 3
CMP_STRIDE = 16
CMP_LEN = 32
CMP_HIDDEN = 128
SEL_BLOCK = 64
SEL_RATIO = 4
SEL_TOPK = 16
WINDOW = 512
DIFF_HEADS = 4
DIFF_QK = 32
DIFF_V = 64
DIFF_ROW = 128
LRU_WIDTH = 384
LRU_BLOCKS = 6
LRU_BLOCK = 64
CONV_W = 4
LRU_C = 8.0
D_NSA = 384
D_DIFF = 256
N_BUCKETS = 32
MAX_DISTANCE = 128
D_FF = 2816
N_EXPERTS = 8
D_FF_EXPERT = 1408
EPS = 1e-6
NEG_INF = -1e30
TINY = 1e-30
FORCE_SCORE = 1e4
ATT_SCALE = HEAD_DIM ** -0.5
DIFF_SCALE = DIFF_QK ** -0.5
PAGE = 128
LOG2E = 1.4426950408889634

TQ = 256
N_BLK_LANES = 256
M_INIT = -1e20
VMEM_LIMIT = 56 * 1024 * 1024
DEC_PAGES = 16
DEC_KEYS = DEC_PAGES * PAGE

(O_QN, O_QD, O_GATE, O_CMP, O_SEL, O_WIN, O_DIFF, O_XR, O_GR, O_KAUG, O_VSEL, O_KWIN, O_VWIN, O_KDIFF, O_VDIFF,
 N_PROJ) = (0, 768, 1280, 1408, 1664, 1920, 2176, 2688, 3072, 3456, 3712, 3968, 4096, 4352, 4608, 5120)
S_QN, S_GN, S_CMP, S_SEL, S_WIN, S_QD, S_KD, S_VD, S_XR, S_GR, D_IN = (
    0, 384, 402, 658, 914, 1170, 1426, 1682, 1938, 2322, 2706)
N_OUT_IN = 768 + 512 + 384


def _proj_cols():
    c, one = [], []

    def seg(cols, ones=None):
        c.extend(cols)
        one.extend(ones if ones is not None else [0.0] * len(cols))

    for h in range(NSA_HEADS):
        hk = h // NSA_GROUP
        seg([-1] * (hk * 64) + [S_QN + h * 64 + d for d in range(64)] + [-1] * (64 - hk * 64))
    for h in range(DIFF_HEADS):
        o = (h % 2) * 64
        seg([-1] * o + [S_QD + h * 64 + d for d in range(64)] + [-1] * (64 - o))
    seg([S_GN + i for i in range(18)] + [-1] * 110)
    seg(list(range(S_CMP, S_CMP + 256)) + list(range(S_SEL, S_SEL + 256)) + list(range(S_WIN, S_WIN + 256)))
    for h in range(DIFF_HEADS):
        seg([S_KD + h * 64 + d for d in range(64)] + [S_VD + h * 64 + d for d in range(64)])
    seg(list(range(S_XR, S_XR + 384)) + list(range(S_GR, S_GR + 384)))
    seg([-1] * 128 + list(range(S_SEL, S_SEL + 128)))
    for hk in range(NSA_KV_HEADS):
        seg([S_SEL + 128 + hk * 64 + d for d in range(64)] + [-1] * 64, [0.0] * 64 + [1.0] * 64)
    seg(list(range(S_WIN, S_WIN + 128)))
    for hk in range(NSA_KV_HEADS):
        seg([S_WIN + 128 + hk * 64 + d for d in range(64)] + [-1] * 64, [0.0] * 64 + [1.0] * 64)
    seg(list(range(S_KD, S_KD + 256)))
    for h in range(DIFF_HEADS):
        seg([S_VD + h * 64 + d for d in range(64)] + [-1] * 64, [0.0] * 64 + [1.0] * 64)
    assert len(c) == N_PROJ
    return np.asarray(c, np.int32), np.asarray(one, np.float32).reshape(1, N_PROJ)


def _outproj_rows():
    r = []
    for h in range(NSA_HEADS):
        r += [h * 64 + d for d in range(64)] + [-1] * 64
    for h in range(DIFF_HEADS):
        r += [D_NSA + h * 64 + d for d in range(64)] + [-1] * 64
    r += list(range(D_NSA + D_DIFF, 1024))
    assert len(r) == N_OUT_IN
    return np.asarray(r, np.int32)


def _take_cols(w, cols):
    return jnp.where(cols[None, :] >= 0, jnp.take(w, np.maximum(cols, 0), axis=1), 0.0)


def _take_rows(w, rows):
    return jnp.where(rows[:, None] >= 0, jnp.take(w, np.maximum(rows, 0), axis=0), 0.0)


def _params(sem):
    return pltpu.CompilerParams(dimension_semantics=sem, vmem_limit_bytes=VMEM_LIMIT)


def _dot(a, b):
    return jnp.dot(a, b, preferred_element_type=f32)


def _dot_nt(a, b):
    return lax.dot_general(a, b, (((1,), (1,)), ((), ())), preferred_element_type=f32)


def _gelu(x):
    return 0.5 * x * (1.0 + jnp.tanh(math.sqrt(2.0 / math.pi) * (x + 0.044715 * (x * x * x))))


def _sigmoid(x):
    return 1.0 / (1.0 + jnp.exp(-x))


def _silu(x):
    return x * _sigmoid(x)


def _normmod(x, g, sc, sh):
    h = x * lax.rsqrt(jnp.mean(x * x, axis=-1, keepdims=True) + EPS) * g
    return h * (1.0 + sc) + sh


def _rowspec(rows, tm, width):
    if rows == 1:
        return pl.BlockSpec((1, width), lambda i: (0, 0))
    return pl.BlockSpec((tm, width), lambda i: (i, 0))


def _resident(shape):
    return pl.BlockSpec(shape, lambda i: (0,) * len(shape), pipeline_mode=pl.Buffered(1))


def _lane(rows):
    return lax.broadcasted_iota(jnp.int32, (rows, 128), 1)


def _ada_body(c_ref, w_ref, b_ref, o_ref):
    c = c_ref[...]
    o_ref[...] = _dot(_silu(c).astype(bf16), w_ref[...].astype(bf16)) + b_ref[...]


def _ada(c, w, b):
    m, n = c.shape[0], w.shape[1]
    tn = 1536
    return pl.pallas_call(
        _ada_body,
        grid=(n // tn,),
        in_specs=[pl.BlockSpec((m, D_MODEL), lambda j: (0, 0)),
                  pl.BlockSpec((D_MODEL, tn), lambda j: (0, j)),
                  pl.BlockSpec((1, tn), lambda j: (0, j))],
        out_specs=pl.BlockSpec((m, tn), lambda j: (0, j)),
        out_shape=jax.ShapeDtypeStruct((m, n), f32),
        compiler_params=_params(("parallel",)),
        name="ada",
    )(c, w, b.reshape(1, n))


PROJ_OUTS = [(768, bf16), (512, bf16), (128, f32), (256, f32), (256, f32), (256, f32), (512, f32), (384, f32),
             (384, f32), (256, bf16), (256, bf16), (128, bf16), (256, bf16), (256, bf16), (512, bf16)]


def _proj_body(tm, x_ref, g_ref, sc_ref, sh_ref, w_ref, one_ref, qn_ref, qd_ref, gate_ref, cmp_ref, sel_ref,
               win_ref, diff_ref, xr_ref, gr_ref, kaug_ref, vsel_ref, kwin_ref, vwin_ref, kdiff_ref, vdiff_ref):
    h = _normmod(x_ref[...], g_ref[...], sc_ref[...], sh_ref[...])
    pr = _dot(h.astype(bf16), w_ref[...]) + one_ref[...]
    qn_ref[...] = (pr[:, O_QN:O_QD] * (ATT_SCALE * LOG2E)).astype(bf16)
    qd_ref[...] = (pr[:, O_QD:O_GATE] * (DIFF_SCALE * LOG2E)).astype(bf16)
    gate_ref[...] = _sigmoid(pr[:, O_GATE:O_CMP])
    cmp_ref[...] = pr[:, O_CMP:O_SEL]
    sel_ref[...] = pr[:, O_SEL:O_WIN]
    win_ref[...] = pr[:, O_WIN:O_DIFF]
    diff_ref[...] = pr[:, O_DIFF:O_XR]
    xr_ref[...] = pr[:, O_XR:O_GR]
    gr_ref[...] = pr[:, O_GR:O_KAUG]
    t = pl.program_id(0) * tm + lax.broadcasted_iota(jnp.int32, (tm, 128), 0)
    onehot = (_lane(tm) == (t // SEL_BLOCK) % 128).astype(f32)
    kaug_ref[:, 0:128] = onehot.astype(bf16)
    kaug_ref[:, 128:256] = pr[:, O_KAUG + 128:O_VSEL].astype(bf16)
    vsel_ref[...] = pr[:, O_VSEL:O_KWIN].astype(bf16)
    kwin_ref[...] = pr[:, O_KWIN:O_VWIN].astype(bf16)
    vwin_ref[...] = pr[:, O_VWIN:O_KDIFF].astype(bf16)
    kdiff_ref[...] = pr[:, O_KDIFF:O_VDIFF].astype(bf16)
    vdiff_ref[...] = pr[:, O_VDIFF:N_PROJ].astype(bf16)


def _proj(x, g, sc, sh, w, onerow, tm):
    m = x.shape[0]
    return pl.pallas_call(
        functools.partial(_proj_body, tm),
        grid=(m // tm,),
        in_specs=[pl.BlockSpec((tm, D_MODEL), lambda i: (i, 0)),
                  pl.BlockSpec((1, D_MODEL), lambda i: (0, 0)),
                  _rowspec(sc.shape[0], tm, D_MODEL), _rowspec(sh.shape[0], tm, D_MODEL),
                  pl.BlockSpec((D_MODEL, N_PROJ), lambda i: (0, 0)),
                  pl.BlockSpec((1, N_PROJ), lambda i: (0, 0))],
        out_specs=[pl.BlockSpec((tm, wd), lambda i: (i, 0)) for wd, _ in PROJ_OUTS],
        out_shape=[jax.ShapeDtypeStruct((m, wd), dt) for wd, dt in PROJ_OUTS],
        compiler_params=_params(("parallel",)),
        name="proj",
    )(x, g.reshape(1, -1), sc, sh, w, onerow)


CMP_PAGES = 16
CMP_ROWS = CMP_PAGES * 8
CMP_OUT = 384


def _compress_body(feature_major, *refs):
    pages = refs[1:1 + CMP_PAGES]
    wc_ref, pe_ref, b1_ref, w2_ref, o_ref, carry_ref = refs[1 + CMP_PAGES:1 + CMP_PAGES + 6]
    g = pl.program_id(1)

    @pl.when(g == 0)
    def _():
        carry_ref[...] = jnp.zeros_like(carry_ref)

    if feature_major:
        xt_ref = refs[1 + CMP_PAGES + 6]
        for i, p in enumerate(pages):
            for half in range(2):
                xt_ref[half, i * PAGE:(i + 1) * PAGE, :] = p[0, 0, half * 128:(half + 1) * 128, :].T
        a = jnp.zeros((CMP_ROWS, 1024), f32)
        for l in range(CMP_STRIDE):
            rows = jnp.concatenate([xt_ref[half, pl.ds(l, CMP_ROWS, stride=CMP_STRIDE), :] for half in range(2)],
                                   axis=1)
            a = a + _dot(rows.astype(bf16), wc_ref[l * 256:(l + 1) * 256, :])
    else:
        x = jnp.concatenate([p[0, 0] for p in pages], axis=0).astype(bf16)
        a = _dot(x, wc_ref[...])
    pe = _dot(pe_ref[...].astype(bf16), wc_ref[...])
    const = pe[0:1, :512] + pe[1:2, 512:] + b1_ref[...]
    first, second = a[:, :512], a[:, 512:]
    rid = lax.broadcasted_iota(jnp.int32, (CMP_ROWS, 512), 0)
    prev_first = jnp.where(rid == 0, carry_ref[0:1, :], pltpu.roll(first, 1, 0))
    carry_ref[0:1, :] = first[CMP_ROWS - 1:CMP_ROWS, :]
    hid = _gelu(prev_first + second + const)
    o_ref[0] = _dot(hid.astype(bf16), w2_ref[...]).astype(bf16)


def _compress(pool, table, wc, pe2, b1, w2bd):
    b, n_pages = table.shape
    steps = n_pages // CMP_PAGES
    feature_major = pool.shape[1:] == (256, PAGE)
    page_shape = (1, 256, PAGE) if feature_major else (1, 8, 4096)

    def page_spec(i):
        return pl.BlockSpec((1,) + page_shape, lambda bb, g, tab: (tab[bb, g * CMP_PAGES + i], 0, 0, 0))

    gs = pltpu.PrefetchScalarGridSpec(
        num_scalar_prefetch=1,
        grid=(b, steps),
        in_specs=[page_spec(i) for i in range(CMP_PAGES)] + [
            pl.BlockSpec((4096, 1024), lambda bb, g, tab: (0, 0)),
            pl.BlockSpec((8, 4096), lambda bb, g, tab: (0, 0)),
            pl.BlockSpec((1, 512), lambda bb, g, tab: (0, 0)),
            pl.BlockSpec((512, CMP_OUT), lambda bb, g, tab: (0, 0))],
        out_specs=pl.BlockSpec((1, CMP_ROWS, CMP_OUT), lambda bb, g, tab: (bb, g, 0)),
        scratch_shapes=[pltpu.VMEM((8, 512), f32)] + (
            [pltpu.VMEM((2, CMP_PAGES * PAGE, 128), f32)] if feature_major else []),
    )
    pool4 = pool.reshape((pool.shape[0],) + page_shape)
    return pl.pallas_call(
        functools.partial(_compress_body, feature_major),
        grid_spec=gs,
        out_shape=jax.ShapeDtypeStruct((b, n_pages * 8, CMP_OUT), bf16),
        compiler_params=_params(("parallel", "arbitrary")),
        name="compress",
    )(table, *([pool4] * CMP_PAGES), wc, pe2, b1, w2bd)


def _compress_weights(cmp_pe, cmp_w1, cmp_b1, cmp_w2):
    w1 = cmp_w1.reshape(2, 2, CMP_STRIDE, HEAD_DIM, CMP_HIDDEN)
    eye4 = jnp.eye(4, dtype=f32)
    w1s = jnp.stack([w1[0], w1[0], w1[1], w1[1]], axis=0)
    wc = jnp.einsum('sflde,st->lsdfte', w1s, eye4).reshape(4096, 1024)
    pe = cmp_pe.reshape(2, 2, CMP_STRIDE, HEAD_DIM)
    pes = jnp.stack([pe[0], pe[0], pe[1], pe[1]], axis=0)
    pe2 = jnp.transpose(pes, (1, 2, 0, 3)).reshape(2, 4096)
    pe2 = jnp.concatenate([pe2, jnp.zeros((6, 4096), f32)], axis=0)
    b1 = jnp.stack([cmp_b1[0], cmp_b1[0], cmp_b1[1], cmp_b1[1]], axis=0).reshape(1, 512)
    w2bd = jnp.zeros((512, CMP_OUT), f32)
    for s, lane0 in ((0, 0), (1, 64), (2, 128), (3, 256)):
        w2bd = w2bd.at[s * 128:(s + 1) * 128, lane0:lane0 + 64].set(cmp_w2[s // 2])
    return wc.astype(bf16), pe2, b1, w2bd.astype(bf16)


def _lane_tile(x, reps):
    return x if reps == 1 else jnp.concatenate([x] * reps, axis=1)


def _flash_tile(s, cbm, vt, m_ref, acc_ref, idx, l_ref=None, v_transposed=False):
    m_prev = m_ref[idx]
    m_new = jnp.maximum(m_prev, jnp.max(s, axis=1, keepdims=True) + cbm)
    alpha = jnp.exp2(m_prev - m_new)
    p = jnp.exp2(s - _lane_tile(m_new - cbm, s.shape[1] // 128))
    if l_ref is not None:
        l_ref[idx] = alpha * l_ref[idx] + jnp.sum(p, axis=1, keepdims=True)
    reps = acc_ref.shape[-1] // 128
    a = _lane_tile(alpha, reps)
    pv = _dot_nt(p.astype(bf16), vt) if v_transposed else _dot(p.astype(bf16), vt)
    acc_ref[idx] = a * acc_ref[idx] + pv
    m_ref[idx] = m_new


def _init_state(m_ref, acc_ref, l_ref=None):
    m_ref[...] = jnp.full(m_ref.shape, M_INIT, f32)
    acc_ref[...] = jnp.zeros(acc_ref.shape, f32)
    if l_ref is not None:
        l_ref[...] = jnp.zeros(l_ref.shape, f32)


def _norm_low(acc):
    l = pltpu.roll(acc, 64, 1)
    return jnp.where(_lane(acc.shape[0]) < 64, acc / jnp.maximum(l, TINY), 0.0)


def _cmp_body(tq, pos0, n_pick, qn_ref, gate_ref, slab_ref, bd_ref, m5_ref, oc_ref, mneg_ref):
    i = pl.program_id(1)
    n_cmp = slab_ref.shape[1]
    t = pos0 + i * tq + lax.broadcasted_iota(jnp.int32, (tq, n_cmp), 0)
    n = lax.broadcasted_iota(jnp.int32, (tq, n_cmp), 1)
    dist = t - (CMP_STRIDE * n + CMP_STRIDE - 1)
    valid = (dist >= 0) & (n >= 1)
    idx = jnp.clip(dist, 0, 127)
    low = _lane(tq) < 64
    blk = lax.broadcasted_iota(jnp.int32, (tq, N_BLK_LANES), 1)
    blkf = blk.astype(f32)
    tb = pos0 + i * tq + lax.broadcasted_iota(jnp.int32, (tq, N_BLK_LANES), 0)
    forced = (blk == tb // SEL_BLOCK) | (blk == 0)
    allowed = blk * SEL_BLOCK <= tb
    small = tq % 16 != 0
    kc = slab_ref[0, :, 0:128]
    kc = kc.astype(f32) if small else kc
    scores = []
    for hk in range(NSA_KV_HEADS):
        vc = slab_ref[0, :, 128 + hk * 128:256 + hk * 128]
        imp = jnp.zeros((tq, n_cmp), f32)
        for g in range(NSA_GROUP):
            h = hk * NSA_GROUP + g
            q = qn_ref[0, :, h * 128:(h + 1) * 128]
            s = _dot_nt(q.astype(f32) if small else q, kc)
            tab = jnp.broadcast_to(bd_ref[h:h + 1, :], (tq, 128))
            bias = jnp.concatenate(
                [jnp.take_along_axis(tab, idx[:, c * 128:(c + 1) * 128], axis=1) for c in range(n_cmp // 128)],
                axis=1)
            s = jnp.where(valid, s + bias, NEG_INF)
            p = jnp.exp2(s - jnp.max(s, axis=1, keepdims=True)) * valid.astype(f32)
            p = p / jnp.maximum(jnp.sum(p, axis=1, keepdims=True), TINY)
            imp = imp + p
            o = _dot(p, vc.astype(f32)) if small else _dot(p.astype(bf16), vc)
            gate = gate_ref[0, :, h * 3:h * 3 + 1]
            oc_ref[0, :, h * 128:(h + 1) * 128] = jnp.where(low, o * gate, 0.0)
        imp_hi = imp.astype(bf16)
        imp_lo = (imp - imp_hi.astype(f32)).astype(bf16)
        if small:
            m5 = m5_ref[...].astype(f32)
            p_slc = _dot(imp_hi.astype(f32), m5) + _dot(imp_lo.astype(f32), m5)
        else:
            p_slc = _dot(imp_hi, m5_ref[...]) + _dot(imp_lo, m5_ref[...])
        scores.append(jnp.where(forced, FORCE_SCORE, jnp.where(allowed, p_slc, -FORCE_SCORE)))
    score = jnp.concatenate(scores, axis=0)
    blk2 = jnp.concatenate([blkf] * NSA_KV_HEADS, axis=0)
    chosen = jnp.zeros(score.shape, jnp.bool_)
    for _ in range(n_pick):
        mx = jnp.max(score, axis=1, keepdims=True)
        first = jnp.min(jnp.where(score == mx, blk2, 4096.0), axis=1, keepdims=True)
        pick = blk2 == first
        chosen = chosen | pick
        score = jnp.where(pick, -3e38, score)
    mneg = jnp.where(chosen, 0.0, NEG_INF)
    for hk in range(NSA_KV_HEADS):
        mneg_ref[0, :, hk * 256:(hk + 1) * 256] = mneg[hk * tq:(hk + 1) * tq].astype(bf16)


def _cmp_attn(qn, gates, slab, bd, m5, tq, pos0):
    b, t, _ = qn.shape
    n_cmp = slab.shape[1]
    n_pick = SEL_TOPK if (pos0 + t - 1) // SEL_BLOCK < N_BLK_LANES else SEL_TOPK - 1
    return pl.pallas_call(
        functools.partial(_cmp_body, tq, pos0, n_pick),
        grid=(b, t // tq),
        in_specs=[pl.BlockSpec((1, tq, 768), lambda bb, i: (bb, i, 0)),
                  pl.BlockSpec((1, tq, 128), lambda bb, i: (bb, i, 0)),
                  pl.BlockSpec((1, n_cmp, CMP_OUT), lambda bb, i: (bb, 0, 0)),
                  pl.BlockSpec((16, 128), lambda bb, i: (0, 0)),
                  pl.BlockSpec((n_cmp, N_BLK_LANES), lambda bb, i: (0, 0))],
        out_specs=[pl.BlockSpec((1, tq, 768), lambda bb, i: (bb, i, 0)),
                   pl.BlockSpec((1, tq, 512), lambda bb, i: (bb, i, 0))],
        out_shape=[jax.ShapeDtypeStruct((b, t, 768), f32), jax.ShapeDtypeStruct((b, t, 512), bf16)],
        compiler_params=_params(("parallel", "parallel")),
        name="cmp_attn",
    )(qn, gates, slab, bd, m5)


def _stack_q(qn_ref, hk):
    return jnp.concatenate([qn_ref[:, (hk * NSA_GROUP + g) * 128:(hk * NSA_GROUP + g + 1) * 128]
                            for g in range(NSA_GROUP)], axis=0)


def _stack_rows(ref, hk):
    return jnp.concatenate([ref[hk * NSA_GROUP + g] for g in range(NSA_GROUP)], axis=0)


def _far_rows(far_ref, hk):
    return jnp.concatenate([jnp.broadcast_to(far_ref[0:1, hk * NSA_GROUP + g:hk * NSA_GROUP + g + 1], (TQ, 128))
                            for g in range(NSA_GROUP)], axis=0)


def _sel_body(qn_ref, mneg_ref, gate_ref, kaug_ref, v_ref, d0_ref, d1_ref, far_ref, os_ref,
              lhs_ref, cb_ref, m_ref, acc_ref):
    i = pl.program_id(0)
    _init_state(m_ref, acc_ref)
    for hk in range(NSA_KV_HEADS):
        q = _stack_q(qn_ref, hk)
        mn = mneg_ref[:, hk * 256:(hk + 1) * 256]
        lhs_ref[hk, 0] = jnp.concatenate([jnp.concatenate([mn[:, :128]] * NSA_GROUP, axis=0), q], axis=1)
        lhs_ref[hk, 1] = jnp.concatenate([jnp.concatenate([mn[:, 128:]] * NSA_GROUP, axis=0), q], axis=1)
        cb_ref[hk] = _far_rows(far_ref, hk)

    def tile(hk, j, half, bias_ref):
        k0 = pl.multiple_of(j * TQ, TQ)
        s = _dot_nt(lhs_ref[hk, half], kaug_ref[pl.ds(k0, TQ), :])
        vt = v_ref[pl.ds(k0, TQ), hk * 128:(hk + 1) * 128]
        if bias_ref is None:
            _flash_tile(s, cb_ref[hk], vt, m_ref, acc_ref, hk)
        else:
            _flash_tile(s + _stack_rows(bias_ref, hk), 0.0, vt, m_ref, acc_ref, hk)

    n_far = jnp.maximum(i - 1, 0)
    split = 128 * SEL_BLOCK // TQ

    def far(half):
        def body(j, c):
            for hk in range(NSA_KV_HEADS):
                tile(hk, j, half, None)
            return c
        return body

    lax.fori_loop(0, jnp.minimum(n_far, split), far(0), 0)
    lax.fori_loop(split, jnp.maximum(n_far, split), far(1), 0)

    def near(j, bias_ref):
        for half in range(2):
            @pl.when((j >= split) == (half == 1))
            def _():
                for hk in range(NSA_KV_HEADS):
                    tile(hk, j, half, bias_ref)

    @pl.when(i >= 1)
    def _():
        near(i - 1, d1_ref)

    near(i, d0_ref)
    for hk in range(NSA_KV_HEADS):
        o = _norm_low(acc_ref[hk])
        for g in range(NSA_GROUP):
            h = hk * NSA_GROUP + g
            os_ref[:, h * 128:(h + 1) * 128] = o[g * TQ:(g + 1) * TQ] * gate_ref[:, h * 3 + 1:h * 3 + 2]


def _sel_attn(qn, mneg, gates, kaug, v, d0, d1, far):
    t = qn.shape[0]
    rows = NSA_GROUP * TQ
    tile_spec = pl.BlockSpec((NSA_HEADS, TQ, TQ), lambda i: (0, 0, 0))
    return pl.pallas_call(
        _sel_body,
        grid=(t // TQ,),
        in_specs=[pl.BlockSpec((TQ, 768), lambda i: (i, 0)),
                  pl.BlockSpec((TQ, 512), lambda i: (i, 0)),
                  pl.BlockSpec((TQ, 128), lambda i: (i, 0)),
                  _resident((t, 256)), _resident((t, 256)),
                  tile_spec, tile_spec,
                  pl.BlockSpec((8, 128), lambda i: (0, 0))],
        out_specs=pl.BlockSpec((TQ, 768), lambda i: (i, 0)),
        out_shape=jax.ShapeDtypeStruct((t, 768), f32),
        scratch_shapes=[pltpu.VMEM((NSA_KV_HEADS, 2, rows, 256), bf16),
                        pltpu.VMEM((NSA_KV_HEADS, rows, 128), f32),
                        pltpu.VMEM((NSA_KV_HEADS, rows, 128), f32),
                        pltpu.VMEM((NSA_KV_HEADS, rows, 128), f32)],
        compiler_params=_params(("parallel",)),
        name="sel_attn",
    )(qn, mneg, gates, kaug, v, d0, d1, far)


def _win_body(qn_ref, gate_ref, k_ref, v_ref, d0_ref, d1_ref, d2_ref, ow_ref, m_ref, acc_ref):
    i = pl.program_id(0)
    _init_state(m_ref, acc_ref)

    def tiles(j, bias_ref):
        k0 = pl.multiple_of(j * TQ, TQ)
        kt = k_ref[pl.ds(k0, TQ), :]
        for hk in range(NSA_KV_HEADS):
            s = _dot_nt(_stack_q(qn_ref, hk), kt) + _stack_rows(bias_ref, hk)
            _flash_tile(s, 0.0, v_ref[pl.ds(k0, TQ), hk * 128:(hk + 1) * 128], m_ref, acc_ref, hk)

    @pl.when(i >= 2)
    def _():
        tiles(i - 2, d2_ref)

    @pl.when(i >= 1)
    def _():
        tiles(i - 1, d1_ref)

    tiles(i, d0_ref)
    for hk in range(NSA_KV_HEADS):
        o = _norm_low(acc_ref[hk])
        for g in range(NSA_GROUP):
            h = hk * NSA_GROUP + g
            ow_ref[:, h * 128:(h + 1) * 128] = o[g * TQ:(g + 1) * TQ] * gate_ref[:, h * 3 + 2:h * 3 + 3]


def _win_attn(qn, gates, k, v, d0, d1, d2):
    t = qn.shape[0]
    rows = NSA_GROUP * TQ
    tile_spec = pl.BlockSpec((NSA_HEADS, TQ, TQ), lambda i: (0, 0, 0))
    return pl.pallas_call(
        _win_body,
        grid=(t // TQ,),
        in_specs=[pl.BlockSpec((TQ, 768), lambda i: (i, 0)),
                  pl.BlockSpec((TQ, 128), lambda i: (i, 0)),
                  _resident((t, 128)), _resident((t, 256)),
                  tile_spec, tile_spec, tile_spec],
        out_specs=pl.BlockSpec((TQ, 768), lambda i: (i, 0)),
        out_shape=jax.ShapeDtypeStruct((t, 768), f32),
        scratch_shapes=[pltpu.VMEM((NSA_KV_HEADS, rows, 128), f32), pltpu.VMEM((NSA_KV_HEADS, rows, 128), f32)],
        compiler_params=_params(("parallel",)),
        name="win_attn",
    )(qn, gates, k, v, d0, d1, d2)


def _diff_lambda(dl, lam_init):
    return (jnp.exp(jnp.sum(dl[0:1] * dl[1:2], axis=1, keepdims=True))
            - jnp.exp(jnp.sum(dl[2:3] * dl[3:4], axis=1, keepdims=True)) + lam_init)


def _subln(od, lam_init, sg):
    y = od * lax.rsqrt(jnp.sum(od * od, axis=1, keepdims=True) * (1.0 / DIFF_V) + EPS) * sg
    return y * (1.0 - lam_init)


def _diff_body(lam_init, qd_ref, k_ref, v_ref, d0_ref, d1_ref, far_ref, dl_ref, sg_ref, od_ref, q_ref, m_ref, acc_ref):
    i = pl.program_id(0)
    lane = _lane(TQ)
    lam = _diff_lambda(dl_ref[...], lam_init)
    n_far = jnp.maximum(i - 1, 0)
    j1 = jnp.maximum(i - 1, 0)
    for h in range(DIFF_HEADS):
        qc = qd_ref[:, h * 128:(h + 1) * 128]
        o = (h % 2) * 64
        zero = jnp.zeros_like(qc)
        q_ref[h] = jnp.concatenate([jnp.where((lane >= o) & (lane < o + DIFF_QK), qc, zero),
                                    jnp.where((lane >= o + DIFF_QK) & (lane < o + 64), qc, zero)], axis=0)
    m_ref[...] = jnp.full(m_ref.shape, M_INIT, f32)
    acc_ref[...] = jnp.zeros(acc_ref.shape, f32)

    def logits(h, j):
        k0 = pl.multiple_of(j * TQ, TQ)
        return _dot_nt(q_ref[h], k_ref[pl.ds(k0, TQ), (h // 2) * 128:(h // 2 + 1) * 128])

    def near_logits(h):
        b1, b0 = d1_ref[h], d0_ref[h]
        s1 = jnp.where(i >= 1, logits(h, j1) + jnp.concatenate([b1, b1], axis=0), NEG_INF)
        s0 = logits(h, i) + jnp.concatenate([b0, b0], axis=0)
        return s1, s0

    def far_max(j, c):
        for h in range(DIFF_HEADS):
            s = logits(h, j)
            m_ref[h] = jnp.maximum(m_ref[h], jnp.maximum(s[:, :128], s[:, 128:]))
        return c

    lax.fori_loop(0, n_far, far_max, 0)
    for h in range(DIFF_HEADS):
        s1, s0 = near_logits(h)
        m = jnp.maximum(jnp.max(m_ref[h], axis=1, keepdims=True) + far_ref[0:1, h:h + 1],
                        jnp.maximum(jnp.max(s1, axis=1, keepdims=True), jnp.max(s0, axis=1, keepdims=True)))
        m_ref[h] = jnp.broadcast_to(m, (2 * TQ, 128))

    def accumulate(h, s, j):
        k0 = pl.multiple_of(j * TQ, TQ)
        p = jnp.exp2(s)
        acc_ref[h] += _dot(p.astype(bf16), v_ref[pl.ds(k0, TQ), h * 128:(h + 1) * 128])

    def far_acc(j, c):
        for h in range(DIFF_HEADS):
            shift = m_ref[h] - far_ref[0:1, h:h + 1]
            accumulate(h, logits(h, j) - _lane_tile(shift, TQ // 128), j)
        return c

    lax.fori_loop(0, n_far, far_acc, 0)
    for h in range(DIFF_HEADS):
        s1, s0 = near_logits(h)
        m = _lane_tile(m_ref[h], TQ // 128)
        accumulate(h, s1 - m, j1)
        accumulate(h, s0 - m, i)
    for h in range(DIFF_HEADS):
        o = _norm_low(acc_ref[h])
        od_ref[:, h * 128:(h + 1) * 128] = _subln(o[:TQ] - lam * o[TQ:], lam_init, sg_ref[...])


def _diff_attn(qd, k, v, d0, d1, far, dl, sg, lam_init):
    t = qd.shape[0]
    tile_spec = pl.BlockSpec((DIFF_HEADS, TQ, TQ), lambda i: (0, 0, 0))
    return pl.pallas_call(
        functools.partial(_diff_body, lam_init),
        grid=(t // TQ,),
        in_specs=[pl.BlockSpec((TQ, 512), lambda i: (i, 0)),
                  _resident((t, 256)), _resident((t, 512)),
                  tile_spec, tile_spec,
                  pl.BlockSpec((8, 128), lambda i: (0, 0)),
                  pl.BlockSpec((4, DIFF_QK), lambda i: (0, 0)),
                  pl.BlockSpec((1, 128), lambda i: (0, 0))],
        out_specs=pl.BlockSpec((TQ, 512), lambda i: (i, 0)),
        out_shape=jax.ShapeDtypeStruct((t, 512), f32),
        scratch_shapes=[pltpu.VMEM((DIFF_HEADS, 2 * TQ, 128), bf16),
                        pltpu.VMEM((DIFF_HEADS, 2 * TQ, 128), f32),
                        pltpu.VMEM((DIFF_HEADS, 2 * TQ, 128), f32)],
        compiler_params=_params(("parallel",)),
        name="diff_attn",
    )(qd, k, v, d0, d1, far, dl, sg)


def _rows48(ref):
    x = ref[0].astype(f32)
    return jnp.concatenate([x[:, h * 128:(h + 1) * 128] for h in range(NSA_HEADS)], axis=0)


def _pad_keys(new):
    return jnp.concatenate([new, jnp.zeros((PAGE - new.shape[0], new.shape[1]), f32)], axis=0)


def _sel_dec_body(*refs):
    pages = refs[1:1 + DEC_PAGES]
    (qn_ref, mneg_ref, gate_ref, new_ref, z_ref, blast_ref, bnew_ref, far_ref, os_ref,
     m_ref, l_ref, acc_ref) = refs[1 + DEC_PAGES:]
    g = pl.program_id(1)
    last = pl.num_programs(1) - 1

    @pl.when(g == 0)
    def _():
        _init_state(m_ref, acc_ref, l_ref)

    q = _rows48(qn_ref).astype(bf16)
    mn = mneg_ref[0].astype(f32)
    mrows = jnp.concatenate([mn[:, (h // NSA_GROUP) * 256:(h // NSA_GROUP + 1) * 256] for h in range(NSA_HEADS)],
                            axis=0).astype(bf16)
    k_t = jnp.concatenate([p[0, 0] for p in pages], axis=1).astype(bf16)
    v_t = jnp.concatenate([p[0, 1] for p in pages], axis=1).astype(bf16)
    blocks_per_step = DEC_KEYS // SEL_BLOCK
    z = z_ref[pl.ds(pl.multiple_of(N_BLK_LANES - g * blocks_per_step, blocks_per_step), N_BLK_LANES), :]
    cb = jnp.concatenate([jnp.broadcast_to(far_ref[0:1, h:h + 1], (8, DEC_KEYS)) for h in range(NSA_HEADS)], axis=0)
    s = _dot(q, k_t) + _dot(mrows, z) + jnp.where(g == last, blast_ref[...], cb)
    _flash_tile(s, 0.0, v_t, m_ref, acc_ref, 0, l_ref, v_transposed=True)

    @pl.when(g == last)
    def _():
        kn = _pad_keys(new_ref[0]).astype(bf16)
        _flash_tile(_dot_nt(q, kn[:, 0:128]) + bnew_ref[...], 0.0, kn[:, 128:256], m_ref, acc_ref, 0, l_ref)
        o = acc_ref[0] / jnp.maximum(l_ref[0], TINY)
        low = _lane(8) < 64
        for h in range(NSA_HEADS):
            oh = o[h * 8:(h + 1) * 8]
            if h // NSA_GROUP == 1:
                oh = pltpu.roll(oh, 64, 1)
            os_ref[0, :, h * 128:(h + 1) * 128] = jnp.where(low, oh * gate_ref[0, :, h * 3 + 1:h * 3 + 2], 0.0)


def _paged_specs(page_shape):
    def page_spec(i):
        return pl.BlockSpec((1,) + page_shape, lambda bb, g, tab: (tab[bb, g * DEC_PAGES + i], 0, 0, 0))
    return [page_spec(i) for i in range(DEC_PAGES)]


def _sel_dec(pool, table, qn, mneg, gates, new, zbig, blast, bnew, far):
    b, n_pages = table.shape
    c3 = lambda wd: pl.BlockSpec((1, 8, wd), lambda bb, g, tab: (bb, 0, 0))
    full = lambda a: pl.BlockSpec(a.shape, lambda bb, g, tab: (0,) * a.ndim)
    gs = pltpu.PrefetchScalarGridSpec(
        num_scalar_prefetch=1,
        grid=(b, n_pages // DEC_PAGES),
        in_specs=_paged_specs((2, 128, PAGE)) + [c3(768), c3(512), c3(128), c3(256), full(zbig), full(blast),
                                                 full(bnew), full(far)],
        out_specs=c3(768),
        scratch_shapes=[pltpu.VMEM((1, 48, 128), f32), pltpu.VMEM((1, 48, 128), f32), pltpu.VMEM((1, 48, 128), f32)],
    )
    return pl.pallas_call(
        _sel_dec_body,
        grid_spec=gs,
        out_shape=jax.ShapeDtypeStruct((b, 8, 768), f32),
        compiler_params=_params(("parallel", "arbitrary")),
        name="sel_dec",
    )(table, *([pool] * DEC_PAGES), qn, mneg, gates, new, zbig, blast, bnew, far)


def _win_dec_body(qn_ref, gate_ref, past_ref, new_ref, bias_ref, ow_ref, keep_ref):
    q = _rows48(qn_ref).astype(bf16)
    past = past_ref[0]
    n_past = past.shape[0]
    kv = jnp.concatenate([past, _pad_keys(new_ref[0])], axis=0).astype(bf16)
    s = _dot_nt(q, kv[:, 0:128]) + bias_ref[...]
    p = jnp.exp2(s - jnp.max(s, axis=1, keepdims=True))
    o = _dot(p.astype(bf16), kv[:, 128:256]) / jnp.maximum(jnp.sum(p, axis=1, keepdims=True), TINY)
    low = _lane(8) < 64
    for h in range(NSA_HEADS):
        oh = o[h * 8:(h + 1) * 8]
        if h // NSA_GROUP == 1:
            oh = pltpu.roll(oh, 64, 1)
        ow_ref[0, :, h * 128:(h + 1) * 128] = jnp.where(low, oh * gate_ref[0, :, h * 3 + 2:h * 3 + 3], 0.0)
    keep_ref[0, 0:n_past - 8, :] = past[8:, :]
    keep_ref[0, n_past - 8:n_past, :] = new_ref[0]


def _win_dec(qn, gates, past, new, bias):
    b, n_past, _ = past.shape
    c3 = lambda wd: pl.BlockSpec((1, 8, wd), lambda bb: (bb, 0, 0))
    return pl.pallas_call(
        _win_dec_body,
        grid=(b,),
        in_specs=[c3(768), c3(128), pl.BlockSpec((1, n_past, 256), lambda bb: (bb, 0, 0)), c3(256),
                  pl.BlockSpec(bias.shape, lambda bb: (0, 0))],
        out_specs=[c3(768), pl.BlockSpec((1, n_past, 256), lambda bb: (bb, 0, 0))],
        out_shape=[jax.ShapeDtypeStruct((b, 8, 768), f32), jax.ShapeDtypeStruct((b, n_past, 256), f32)],
        compiler_params=_params(("parallel",)),
        name="win_dec",
    )(qn, gates, past, new, bias)


def _diff_dec_body(lam_init, *refs):
    pages = refs[1:1 + DEC_PAGES]
    (qd_ref, new_ref, blast_ref, bnew_ref, far_ref, dl_ref, sg_ref, od_ref, m_ref, l_ref, acc_ref) = refs[1 + DEC_PAGES:]
    g = pl.program_id(1)
    last = pl.num_programs(1) - 1

    @pl.when(g == 0)
    def _():
        _init_state(m_ref, acc_ref, l_ref)

    lane = _lane(8)
    x = qd_ref[0].astype(f32)
    rows = []
    zero = jnp.zeros((8, 128), f32)
    for h in range(DIFF_HEADS):
        qc = x[:, h * 128:(h + 1) * 128]
        if h % 2 == 1:
            qc = pltpu.roll(qc, 64, 1)
        for half in range(2):
            piece = jnp.where((lane >= half * DIFF_QK) & (lane < (half + 1) * DIFF_QK), qc, 0.0)
            rows.append(jnp.concatenate([zero] * h + [piece] + [zero] * (DIFF_HEADS - 1 - h), axis=1))
    q = jnp.concatenate(rows, axis=0).astype(bf16)
    kv = jnp.concatenate([jnp.concatenate([p[0, :, h, :] for p in pages], axis=0) for h in range(DIFF_HEADS)],
                         axis=1).astype(bf16)
    cb = jnp.concatenate([jnp.broadcast_to(far_ref[0:1, h:h + 1], (16, DEC_KEYS)) for h in range(DIFF_HEADS)], axis=0)
    s = _dot_nt(q, kv) + jnp.where(g == last, blast_ref[...], cb)
    _flash_tile(s, 0.0, kv, m_ref, acc_ref, 0, l_ref)

    @pl.when(g == last)
    def _():
        kn = _pad_keys(new_ref[0]).astype(bf16)
        _flash_tile(_dot_nt(q, kn) + bnew_ref[...], 0.0, kn, m_ref, acc_ref, 0, l_ref)
        lam = _diff_lambda(dl_ref[...], lam_init)
        l = l_ref[0]
        for h in range(DIFF_HEADS):
            a = acc_ref[0, h * 16:(h + 1) * 16, h * 128:(h + 1) * 128] / jnp.maximum(l[h * 16:(h + 1) * 16], TINY)
            od = pltpu.roll(a[0:8] - lam * a[8:16], 64, 1)
            od_ref[0, :, h * 128:(h + 1) * 128] = _subln(jnp.where(lane < 64, od, 0.0), lam_init, sg_ref[...])


def _diff_dec(pool, table, qd, new, blast, bnew, far, dl, sg, lam_init):
    b, n_pages = table.shape
    c3 = lambda wd: pl.BlockSpec((1, 8, wd), lambda bb, g, tab: (bb, 0, 0))
    full = lambda a: pl.BlockSpec(a.shape, lambda bb, g, tab: (0,) * a.ndim)
    gs = pltpu.PrefetchScalarGridSpec(
        num_scalar_prefetch=1,
        grid=(b, n_pages // DEC_PAGES),
        in_specs=_paged_specs((PAGE, DIFF_HEADS, DIFF_ROW)) + [c3(512), c3(512), full(blast), full(bnew), full(far),
                                                               full(dl), full(sg)],
        out_specs=c3(512),
        scratch_shapes=[pltpu.VMEM((1, 64, 128), f32), pltpu.VMEM((1, 64, 128), f32), pltpu.VMEM((1, 64, 512), f32)],
    )
    return pl.pallas_call(
        functools.partial(_diff_dec_body, lam_init),
        grid_spec=gs,
        out_shape=jax.ShapeDtypeStruct((b, 8, 512), f32),
        compiler_params=_params(("parallel", "arbitrary")),
        name="diff_dec",
    )(table, *([pool] * DEC_PAGES), qd, new, blast, bnew, far, dl, sg)


LRU_TT = 256


def _lru_body(tt, x_ref, gi_ref, cbuf_ref, h0_ref, cw_ref, cb_ref, wa_ref, ba_ref, wx_ref, bx_ref, lam_ref,
              y_ref, hl_ref, cn_ref, xe_ref, a_ref, b_ref, h_ref):
    j = pl.program_id(1)

    @pl.when(j == 0)
    def _():
        xe_ref[0:8, :] = jnp.zeros((8, LRU_WIDTH), f32)
        xe_ref[5:8, :] = cbuf_ref[0]
        h_ref[0:1, :] = h0_ref[0]

    x = x_ref[0]
    xe_ref[8:8 + tt, :] = x
    xc = cb_ref[...] + sum(xe_ref[5 + k:5 + k + tt, :] * cw_ref[k:k + 1, :] for k in range(CONV_W))
    xcb = xc.astype(bf16)
    r = _sigmoid(_dot(xcb, wa_ref[...]) + ba_ref[...])
    ig = _sigmoid(_dot(xcb, wx_ref[...]) + bx_ref[...])
    lam = lam_ref[...]
    softplus = jnp.maximum(-lam, 0.0) + jnp.log(1.0 + jnp.exp(-jnp.abs(lam)))
    log_a = -LRU_C * r * softplus
    a = jnp.exp(log_a)
    a_ref[...] = a
    b_ref[...] = jnp.sqrt(1.0 - jnp.exp(2.0 * log_a)) * (ig * xc)

    def step(k, h):
        h = a_ref[pl.ds(k, 1), :] * h + b_ref[pl.ds(k, 1), :]
        b_ref[pl.ds(k, 1), :] = h
        return h

    h = lax.fori_loop(0, tt, step, h_ref[0:1, :])
    h_ref[0:1, :] = h
    y_ref[0] = b_ref[...] * _gelu(gi_ref[0])
    hl_ref[0] = h
    cn_ref[0] = xe_ref[5 + tt:8 + tt, :]
    xe_ref[5:8, :] = xe_ref[5 + tt:8 + tt, :]


def _rglru(x, gate_in, conv_buf, h0, cw, cb, wa, ba, wx, bx, lam):
    b, t, w = x.shape
    tt = min(LRU_TT, t)

    def bd(wblk):
        m = jnp.zeros((w, w), f32)
        for n in range(LRU_BLOCKS):
            m = m.at[n * LRU_BLOCK:(n + 1) * LRU_BLOCK, n * LRU_BLOCK:(n + 1) * LRU_BLOCK].set(wblk[n])
        return m.astype(bf16)

    vec = lambda: pl.BlockSpec((1, w), lambda bb, j: (0, 0))
    return pl.pallas_call(
        functools.partial(_lru_body, tt),
        grid=(b, t // tt),
        in_specs=[pl.BlockSpec((1, tt, w), lambda bb, j: (bb, j, 0)),
                  pl.BlockSpec((1, tt, w), lambda bb, j: (bb, j, 0)),
                  pl.BlockSpec((1, 3, w), lambda bb, j: (bb, 0, 0)),
                  pl.BlockSpec((1, 1, w), lambda bb, j: (bb, 0, 0)),
                  pl.BlockSpec((CONV_W, w), lambda bb, j: (0, 0)), vec(),
                  pl.BlockSpec((w, w), lambda bb, j: (0, 0)), vec(),
                  pl.BlockSpec((w, w), lambda bb, j: (0, 0)), vec(), vec()],
        out_specs=[pl.BlockSpec((1, tt, w), lambda bb, j: (bb, j, 0)),
                   pl.BlockSpec((1, 1, w), lambda bb, j: (bb, 0, 0)),
                   pl.BlockSpec((1, 3, w), lambda bb, j: (bb, 0, 0))],
        out_shape=[jax.ShapeDtypeStruct((b, t, w), f32), jax.ShapeDtypeStruct((b, 1, w), f32),
                   jax.ShapeDtypeStruct((b, 3, w), f32)],
        scratch_shapes=[pltpu.VMEM((tt + 8, w), f32), pltpu.VMEM((tt, w), f32), pltpu.VMEM((tt, w), f32),
                        pltpu.VMEM((8, w), f32)],
        compiler_params=_params(("parallel", "arbitrary")),
        name="rglru",
    )(x, gate_in, conv_buf, h0.reshape(b, 1, w), cw, cb.reshape(1, w), bd(wa), ba.reshape(1, w), bd(wx),
      bx.reshape(1, w), lam.reshape(1, w))


def _outproj_body(x_ref, gt_ref, oc_ref, os_ref, ow_ref, od_ref, yr_ref, w_ref, o_ref):
    on = oc_ref[...] + os_ref[...] + ow_ref[...]
    cat = jnp.concatenate([on.astype(bf16), od_ref[...].astype(bf16), yr_ref[...].astype(bf16)], axis=1)
    o_ref[...] = x_ref[...] + gt_ref[...] * _dot(cat, w_ref[...])


def _outproj(x, gt, oc, os_, ow, od, yr, w, tm):
    m = x.shape[0]
    row = lambda wd: pl.BlockSpec((tm, wd), lambda i: (i, 0))
    return pl.pallas_call(
        _outproj_body,
        grid=(m // tm,),
        in_specs=[row(D_MODEL), _rowspec(gt.shape[0], tm, D_MODEL), row(768), row(768), row(768), row(512),
                  row(384), pl.BlockSpec((N_OUT_IN, D_MODEL), lambda i: (0, 0))],
        out_specs=row(D_MODEL),
        out_shape=jax.ShapeDtypeStruct((m, D_MODEL), f32),
        compiler_params=_params(("parallel",)),
        name="outproj",
    )(x, gt, oc, os_, ow, od, yr, w)


def _ffn_body(n_exp, final, x_ref, g_ref, sc_ref, sh_ref, gt_ref, rw_ref, rb_ref, w1_ref, w3_ref, w2_ref, fg_ref,
              o_ref, h_ref, acc_ref, gate_ref):
    e, f = pl.program_id(1), pl.program_id(2)

    @pl.when((e == 0) & (f == 0))
    def _():
        h = _normmod(x_ref[...], g_ref[...], sc_ref[...], sh_ref[...])
        h_ref[...] = h.astype(bf16)
        acc_ref[...] = jnp.zeros_like(acc_ref)
        if n_exp > 1:
            logits = jnp.dot(h, rw_ref[...], preferred_element_type=f32, precision=lax.Precision.HIGHEST)
            lane = lax.broadcasted_iota(jnp.int32, logits.shape, 1)
            logits = jnp.where(lane < n_exp, logits + rb_ref[...], NEG_INF)
            v1 = jnp.max(logits, axis=1, keepdims=True)
            i1 = jnp.min(jnp.where(logits == v1, lane, 4096), axis=1, keepdims=True)
            rest = jnp.where(lane == i1, NEG_INF, logits)
            v2 = jnp.max(rest, axis=1, keepdims=True)
            i2 = jnp.min(jnp.where(rest == v2, lane, 4096), axis=1, keepdims=True)
            e2 = jnp.exp(v2 - v1)
            w_1 = 1.0 / (1.0 + e2)
            w_2 = e2 / (1.0 + e2)
            gate = jnp.where(lane == i1, w_1, 0.0) + jnp.where(lane == i2, w_2, 0.0)
            for k in range(n_exp):
                gate_ref[k] = gate[:, k:k + 1]

    hb = h_ref[...]
    hid = _silu(_dot(hb, w1_ref[0])) * _dot(hb, w3_ref[0])
    if n_exp > 1:
        hid = hid * gate_ref[e]
    acc_ref[...] += _dot(hid.astype(bf16), w2_ref[0])

    @pl.when((e == n_exp - 1) & (f == pl.num_programs(2) - 1))
    def _():
        y = x_ref[...] + gt_ref[...] * acc_ref[...]
        if final:
            y = y * lax.rsqrt(jnp.mean(y * y, axis=-1, keepdims=True) + EPS) * fg_ref[...]
        o_ref[...] = y


def _ffn(x, g, sc, sh, gt, rw, rb, w1, w3, w2, fg, final, tm, tf):
    m = x.shape[0]
    n_exp, _, ff = w1.shape
    vec = lambda: pl.BlockSpec((1, D_MODEL), lambda i, e, f: (0, 0))

    def rowspec(a):
        if a.shape[0] == 1:
            return vec()
        return pl.BlockSpec((tm, D_MODEL), lambda i, e, f: (i, 0))

    return pl.pallas_call(
        functools.partial(_ffn_body, n_exp, final),
        grid=(m // tm, n_exp, ff // tf),
        in_specs=[pl.BlockSpec((tm, D_MODEL), lambda i, e, f: (i, 0)), vec(), rowspec(sc), rowspec(sh), rowspec(gt),
                  pl.BlockSpec((D_MODEL, 128), lambda i, e, f: (0, 0)),
                  pl.BlockSpec((1, 128), lambda i, e, f: (0, 0)),
                  pl.BlockSpec((1, D_MODEL, tf), lambda i, e, f: (e, 0, f)),
                  pl.BlockSpec((1, D_MODEL, tf), lambda i, e, f: (e, 0, f)),
                  pl.BlockSpec((1, tf, D_MODEL), lambda i, e, f: (e, f, 0)),
                  vec()],
        out_specs=pl.BlockSpec((tm, D_MODEL), lambda i, e, f: (i, 0)),
        out_shape=jax.ShapeDtypeStruct((m, D_MODEL), f32),
        scratch_shapes=[pltpu.VMEM((tm, D_MODEL), bf16), pltpu.VMEM((tm, D_MODEL), f32),
                        pltpu.VMEM((N_EXPERTS, tm, 1), f32)],
        compiler_params=_params(("parallel", "arbitrary", "arbitrary")),
        name="ffn",
    )(x, g.reshape(1, -1), sc, sh, gt, rw, rb, w1, w3, w2, fg.reshape(1, -1))


def _t5_bucket(dist):
    n = jnp.maximum(dist, 0)
    exact = N_BUCKETS // 2
    nf = jnp.maximum(n, 1).astype(f32)
    large = exact + (jnp.log(nf / exact) / math.log(MAX_DISTANCE / exact) * (N_BUCKETS - exact)).astype(jnp.int32)
    return jnp.where(n < exact, n, jnp.minimum(large, N_BUCKETS - 1))


def _by_dist(bd, dist, valid):
    onehot = jax.nn.one_hot(jnp.clip(dist, 0, 127), 128, dtype=f32)
    v = jnp.einsum('...d,dh->h...', onehot, bd, precision=lax.Precision.HIGHEST)
    return jnp.where(valid[None], v, NEG_INF)


def _bias_tables(rel_bias, pos0_dec):
    nh = rel_bias.shape[1]
    bd = rel_bias[_t5_bucket(jnp.arange(128, dtype=jnp.int32))] * LOG2E
    far = rel_bias[N_BUCKETS - 1] * LOG2E
    r = jnp.arange(TQ, dtype=jnp.int32)[:, None]
    c = jnp.arange(TQ, dtype=jnp.int32)[None, :]
    d0 = _by_dist(bd, r - c, r - c >= 0)
    d1 = _by_dist(bd, r - c + TQ, r - c + TQ >= 0)
    d2 = jnp.where((c > r)[None], far[:, None, None], NEG_INF)
    bd_pad = jnp.zeros((16, 128), f32).at[:nh].set(bd.T)
    far_pad = jnp.zeros((8, 128), f32).at[0, :nh].set(far)
    r8 = jnp.arange(8, dtype=jnp.int32)[:, None]
    ck = jnp.arange(DEC_KEYS, dtype=jnp.int32)[None, :]
    last = _by_dist(bd, DEC_KEYS + r8 - ck, jnp.ones((8, DEC_KEYS), bool))
    cn = jnp.arange(PAGE, dtype=jnp.int32)[None, :]
    new = _by_dist(bd, r8 - cn, (r8 - cn >= 0) & (cn < 8))
    n_win = min(WINDOW, pos0_dec)
    cw = jnp.arange(n_win, dtype=jnp.int32)[None, :]
    dw = n_win + r8 - cw
    win = _by_dist(bd, dw, dw < WINDOW)
    return d0, d1, d2, bd_pad, far_pad, far, last, new, win


def _m5(n_cmp):
    n = np.arange(n_cmp)[:, None]
    j = np.arange(N_BLK_LANES)[None, :]
    return jnp.asarray(((n >= 4 * j) & (n <= 4 * j + 4)).astype(np.float32), bf16)


def _zbig():
    rho = np.arange(2 * N_BLK_LANES)[:, None]
    c = np.arange(DEC_KEYS)[None, :]
    return jnp.asarray((rho == N_BLK_LANES + c // SEL_BLOCK).astype(np.float32), bf16)


def _split6(mod):
    return [mod[:, k * D_MODEL:(k + 1) * D_MODEL] for k in range(6)]


def _prompt_layer(x, mod, lw, tb, l, final_g, last):
    t = x.shape[0]
    sh1, sc1, gt1, sh2, sc2, gt2 = _split6(mod)
    (qn, qd, gates, cmp, sel, win, diff, xr, gr, kaug, vsel, kwin, vwin, kdiff, vdiff) = _proj(
        x, lw['norm_mix_g'], sc1, sh1, lw['w_aug'], lw['onerow'], 256)
    n_pages = t // PAGE
    ident = jnp.arange(n_pages, dtype=jnp.int32).reshape(1, n_pages)
    slab = _compress(cmp.reshape(n_pages, 8, 4096), ident, *lw['cmp'])
    oc, mneg = _cmp_attn(qn[None], gates[None], slab, tb['bd_nsa'], tb['m5_p'], TQ, 0)
    os_ = _sel_attn(qn, mneg[0], gates, kaug, vsel, tb['d0n'], tb['d1n'], tb['far_n'])
    ow = _win_attn(qn, gates, kwin, vwin, tb['d0n'], tb['d1n'], tb['d2n'])
    lam_init = 0.8 - 0.6 * math.exp(-0.3 * l)
    od = _diff_attn(qd, kdiff, vdiff, tb['d0d'], tb['d1d'], tb['far_d'], lw['diff_lambda'], lw['subln'], lam_init)
    zeros_c = jnp.zeros((1, CONV_W - 1, LRU_WIDTH), f32)
    zeros_h = jnp.zeros((1, LRU_WIDTH), f32)
    yr, hl, cn = _rglru(xr[None], gr[None], zeros_c, zeros_h, *lw['lru'])
    x = _outproj(x, gt1, oc[0], os_, ow, od, yr[0], lw['w_out'], 512)
    x = _ffn(x, lw['norm_ffn_g'], sc2, sh2, gt2, *lw['ffn'], final_g, last, 512, 1408)
    n_keep = min(WINDOW, t)
    state = (cmp.reshape(1, t, 2, NSA_KV_HEADS, HEAD_DIM), sel.reshape(1, t, 2, NSA_KV_HEADS, HEAD_DIM),
             diff.reshape(1, t, DIFF_HEADS, DIFF_ROW), win[t - n_keep:].reshape(1, n_keep, 2, NSA_KV_HEADS, HEAD_DIM),
             hl.reshape(1, LRU_WIDTH), cn)
    return x, state


def _sample_layer(x, mod, lw, tb, l, n_layers, final_g, last, caches, page_table, nb, dec):
    pool_cmp, pool_sel, pool_diff, win_past, h0, conv0 = caches
    pos0 = page_table.shape[1] * PAGE
    table_all = page_table + l * (pool_cmp.shape[0] // n_layers)
    rep = lambda a: jnp.repeat(a, dec, axis=0)
    sh1, sc1, gt1, sh2, sc2, gt2 = [rep(m) for m in _split6(mod)]
    m = nb * dec
    (qn, qd, gates, cmp, sel, win, diff, xr, gr, _, _, _, _, _, _) = _proj(
        x, lw['norm_mix_g'], sc1, sh1, lw['w_aug'], lw['onerow'], m)
    b3 = lambda a: a.reshape(nb, dec, a.shape[-1])
    slab = _compress(pool_cmp, table_all, *lw['cmp'])
    oc, mneg = _cmp_attn(b3(qn), b3(gates), slab, tb['bd_nsa'], tb['m5_s'], dec, pos0)
    os_ = _sel_dec(pool_sel, table_all, b3(qn), mneg, b3(gates), b3(sel), tb['zbig'], tb['last_n'], tb['new_n'],
                   tb['far_n'])
    ow, win_keep = _win_dec(b3(qn), b3(gates), win_past, b3(win), tb['win_n'])
    lam_init = 0.8 - 0.6 * math.exp(-0.3 * l)
    od = _diff_dec(pool_diff, table_all, b3(qd), b3(diff), tb['last_d'], tb['new_d'], tb['far_d'],
                   lw['diff_lambda'], lw['subln'], lam_init)
    yr, hl, cn = _rglru(b3(xr), b3(gr), conv0, h0, *lw['lru'])
    flat = lambda a: a.reshape(m, a.shape[-1])
    x = _outproj(x, gt1, flat(oc), flat(os_), flat(ow), flat(od), flat(yr), lw['w_out'], m)
    x = _ffn(x, lw['norm_ffn_g'], sc2, sh2, gt2, *lw['ffn'], final_g, last, m, 1408)
    n_keep = win_keep.shape[1]
    state = (cmp.reshape(nb, dec, 2, NSA_KV_HEADS, HEAD_DIM), sel.reshape(nb, dec, 2, NSA_KV_HEADS, HEAD_DIM),
             diff.reshape(nb, dec, DIFF_HEADS, DIFF_ROW), win_keep.reshape(nb, n_keep, 2, NSA_KV_HEADS, HEAD_DIM),
             hl.reshape(nb, LRU_WIDTH), cn)
    return x, state


def kernel(x_prompt, x_sample, c_prompt, c_sample, cache_nsa_cmp, cache_nsa_sel, cache_diff, cache_nsa_win, state_lru_h, state_lru_conv, page_table, rel_bias, norm_mix_g, norm_ffn_g, final_norm_g, w_ada, b_ada, w_in, cmp_pe, cmp_w1, cmp_b1, cmp_w2, diff_lambda, diff_subln_g, lru_conv_w, lru_conv_b, lru_wa, lru_ba, lru_wx, lru_bx, lru_lambda, w_out, ffn_w1, ffn_w3, ffn_w2, router_w, router_b, moe_w1, moe_w3, moe_w2):
    depth = w_in.shape[0]
    t = x_prompt.shape[1]
    nb, dec = x_sample.shape[0], x_sample.shape[1]
    n_pool = cache_nsa_cmp.shape[1]
    pos0_dec = page_table.shape[1] * PAGE
    cols, onerow = _proj_cols()
    rows = _outproj_rows()
    onerow = jnp.asarray(onerow)
    d0, d1, d2, bd, far_pad, far, last, new, win = _bias_tables(rel_bias, pos0_dec)
    nh = NSA_HEADS
    stack_rows = lambda a: a.reshape(-1, a.shape[-1])
    twice = lambda a: jnp.repeat(a, 2, axis=0)
    tb = {'d0n': d0[:nh], 'd1n': d1[:nh], 'd2n': d2[:nh], 'd0d': d0[nh:], 'd1d': d1[nh:], 'bd_nsa': bd,
          'far_n': far_pad, 'far_d': jnp.zeros((8, 128), f32).at[0, :DIFF_HEADS].set(far[nh:]),
          'm5_p': _m5(t // CMP_STRIDE), 'm5_s': _m5(pos0_dec // CMP_STRIDE), 'zbig': _zbig(),
          'last_n': stack_rows(last[:nh]), 'new_n': stack_rows(new[:nh]),
          'win_n': jnp.concatenate([stack_rows(win[:nh]), stack_rows(new[:nh])], axis=1),
          'last_d': stack_rows(twice(last[nh:])), 'new_d': stack_rows(twice(new[nh:]))}

    c_all = jnp.concatenate([c_prompt, c_sample, jnp.zeros((7, D_MODEL), f32)], axis=0)
    layers = []
    for l in range(depth):
        j = l // 2
        lw = {'norm_mix_g': norm_mix_g[l], 'norm_ffn_g': norm_ffn_g[l],
              'w_aug': _take_cols(w_in[l], cols).astype(bf16), 'onerow': onerow,
              'cmp': _compress_weights(cmp_pe[l], cmp_w1[l], cmp_b1[l], cmp_w2[l]),
              'diff_lambda': diff_lambda[l],
              'subln': jnp.zeros((1, 128), f32).at[0, :64].set(diff_subln_g[l]),
              'lru': (lru_conv_w[l], lru_conv_b[l], lru_wa[l], lru_ba[l], lru_wx[l], lru_bx[l], lru_lambda[l]),
              'w_out': _take_rows(w_out[l], rows).astype(bf16)}
        if l % 2 == 0:
            lw['ffn'] = (jnp.zeros((D_MODEL, 128), f32), jnp.zeros((1, 128), f32),
                         ffn_w1[j][None].astype(bf16), ffn_w3[j][None].astype(bf16), ffn_w2[j][None].astype(bf16))
        else:
            rw = jnp.zeros((D_MODEL, 128), f32).at[:, :N_EXPERTS].set(router_w[j])
            rb = jnp.zeros((1, 128), f32).at[0, :N_EXPERTS].set(router_b[j])
            lw['ffn'] = (rw, rb, moe_w1[j].astype(bf16), moe_w3[j].astype(bf16), moe_w2[j].astype(bf16))
        layers.append(lw)

    mods = [_ada(c_all, w_ada[l], b_ada[l]) for l in range(depth)]

    x = x_prompt[0]
    p_states = []
    for l in range(depth):
        x, st = _prompt_layer(x, mods[l][0:1], layers[l], tb, l, final_norm_g, l == depth - 1)
        p_states.append(st)
    y_prompt = x[None]

    xs = x_sample.reshape(nb * dec, D_MODEL)
    s_states = []
    pool_cmp = jnp.transpose(cache_nsa_cmp, (0, 1, 3, 4, 5, 2)).reshape(depth * n_pool, 256, PAGE)
    pool_sel = jnp.transpose(cache_nsa_sel, (0, 1, 3, 4, 5, 2)).reshape(depth * n_pool, 2, 128, PAGE)
    pool_diff = cache_diff.reshape(depth * n_pool, PAGE, DIFF_HEADS, DIFF_ROW)
    for l in range(depth):
        caches = (pool_cmp, pool_sel, pool_diff, cache_nsa_win[l].reshape(nb, -1, 256),
                  state_lru_h[l], state_lru_conv[l])
        xs, st = _sample_layer(xs, mods[l][1:1 + nb], layers[l], tb, l, depth, final_norm_g, l == depth - 1, caches,
                               page_table, nb, dec)
        s_states.append(st)
    y_sample = xs.reshape(nb, dec, D_MODEL)

    stack = lambda sts, k: jnp.stack([s[k] for s in sts], axis=0)
    return (y_prompt, y_sample, *[stack(p_states, k) for k in range(6)], *[stack(s_states, k) for k in range(6)])
```
